```python
import math
import jax, jax.numpy as jnp
from jax import lax
import numpy as np

D_MODEL = 1024
BATCH = 2
SEQ = 8192
DEPTH = 4

CHUNK = 64
MIX_WIDTH = D_MODEL
POOL_WIDTH = D_MODEL // 2
POOL_GROUPS = 4
POOL_GROUP_DIM = POOL_WIDTH // POOL_GROUPS
POOL_WINDOWS = (2, 4, 8, 16)
GDN_WIDTH = MIX_WIDTH - POOL_WIDTH
GDN_HEADS = 4
GDN_HEAD_DIM = GDN_WIDTH // GDN_HEADS
CONV_WIDTH = 4
D_FF = 4 * D_MODEL
EPS = 1e-6
IN_WIDTH = POOL_WIDTH + 4 * GDN_WIDTH + 2 * GDN_HEADS

kernel_name = "hybrid_pool_gdn_trunk"


def rms_norm(x, w):
    xf = x.astype(jnp.float32)
    y = xf * lax.rsqrt(jnp.mean(xf * xf, axis=-1, keepdims=True) + EPS)
    return (y * w.astype(jnp.float32)).astype(x.dtype)


def l2_norm(t):
    return t * lax.rsqrt(jnp.sum(t * t, axis=-1, keepdims=True) + EPS)


def pool_mixer(u, w_pool, pool_scale):
    B, S, _ = u.shape
    uf = u.astype(jnp.float32)
    cs = jnp.cumsum(uf, axis=1)
    count = jnp.arange(1, S + 1, dtype=jnp.float32)[None, :, None]
    outs = []
    for g, w in enumerate(POOL_WINDOWS):
        sl = slice(g * POOL_GROUP_DIM, (g + 1) * POOL_GROUP_DIM)
        c = cs[..., sl]
        prev = jnp.pad(c, ((0, 0), (w, 0), (0, 0)))[:, :S]
        mean = (c - prev) / jnp.minimum(count, float(w))
        outs.append(mean - uf[..., sl])
    pooled = jnp.stack(outs, axis=2).astype(u.dtype)
    mixed = jnp.einsum('bsgc,gcd->bsgd', pooled, w_pool)
    return mixed.reshape(B, S, POOL_WIDTH) * pool_scale


def causal_conv_silu(x, w):
    C = x.shape[-1]
    y = lax.conv_general_dilated(x, w[:, None, :].astype(x.dtype), window_strides=(1,),
                                 padding=[(CONV_WIDTH - 1, 0)],
                                 dimension_numbers=('NWC', 'WIO', 'NWC'),
                                 feature_group_count=C)
    return jax.nn.silu(y)


def gated_delta_net(q, k, v, z, a, b, a_log, dt_bias, gdn_norm):
    B, S, _ = q.shape
    H, Dh, C = GDN_HEADS, GDN_HEAD_DIM, CHUNK
    N = S // C
    f32 = jnp.float32

    def heads(t):
        return t.astype(f32).reshape(B, N, C, H, Dh).transpose(0, 3, 1, 2, 4)

    def per_head(t):
        return t.reshape(B, N, C, H).transpose(0, 3, 1, 2)

    qh = l2_norm(heads(q)) * (Dh ** -0.5)
    kh = l2_norm(heads(k))
    vh = heads(v)
    beta = per_head(jax.nn.sigmoid(b.astype(f32)))
    g = -jnp.exp(a_log.astype(f32)) * jax.nn.softplus(a.astype(f32) + dt_bias.astype(f32))
    gcum = jnp.cumsum(per_head(g), axis=-1)

    idx = jnp.arange(C)
    incl = idx[:, None] >= idx[None, :]
    strict = idx[:, None] > idx[None, :]
    diff = gcum[..., :, None] - gcum[..., None, :]
    decay_incl = jnp.exp(jnp.where(incl, diff, -jnp.inf))
    decay_strict = jnp.where(strict, decay_incl, 0.0)

    kb = kh * beta[..., None]
    m = jnp.einsum('bhncd,bhnjd->bhncj', kb, kh) * decay_strict
    lhs = jnp.eye(C, dtype=f32) + m
    rhs = jnp.concatenate([vh * beta[..., None], kb * jnp.exp(gcum)[..., None]], axis=-1)
    sol = lax.linalg.triangular_solve(lhs, rhs, left_side=True, lower=True)
    value, k_cumdecay = sol[..., :Dh], sol[..., Dh:]
    attn_inner = jnp.einsum('bhncd,bhnjd->bhncj', qh, kh) * decay_incl
    q_dec = qh * jnp.exp(gcum)[..., None]
    g_last = gcum[..., -1]
    k_dec = kh * jnp.exp(g_last[..., None] - gcum)[..., None]

    def step(state, inp):
        qd, kc, val, ai, kd, gl = inp
        v_new = val - jnp.einsum('bhcd,bhde->bhce', kc, state)
        o = jnp.einsum('bhcd,bhde->bhce', qd, state) + jnp.einsum('bhcj,bhje->bhce', ai, v_new)
        state = state * jnp.exp(gl)[..., None, None] + jnp.einsum('bhcd,bhce->bhde', kd, v_new)
        return state, o

    xs = (jnp.moveaxis(q_dec, 2, 0), jnp.moveaxis(k_cumdecay, 2, 0), jnp.moveaxis(value, 2, 0),
          jnp.moveaxis(attn_inner, 2, 0), jnp.moveaxis(k_dec, 2, 0), jnp.moveaxis(g_last, 2, 0))
    state0 = jnp.zeros((B, H, Dh, Dh), f32)
    _, o = lax.scan(step, state0, xs)
    o = o.transpose(1, 0, 3, 2, 4).reshape(B, S, H, Dh)
    o = o * lax.rsqrt(jnp.mean(o * o, axis=-1, keepdims=True) + EPS) * gdn_norm.astype(f32)
    o = o * jax.nn.silu(z.astype(f32).reshape(B, S, H, Dh))
    return o.reshape(B, S, GDN_WIDTH).astype(q.dtype)


def setup_inputs(seed: int = 0) -> dict:
    key = jax.random.key(seed)
    ks = jax.random.split(key, 16)
    nrm = jax.random.normal
    H, Dh = GDN_HEADS, GDN_HEAD_DIM
    x = nrm(ks[0], (BATCH, SEQ, D_MODEL), jnp.float32)
    norm_mix = 1.0 + 0.02 * nrm(ks[1], (DEPTH, D_MODEL), jnp.float32)
    w_in = nrm(ks[2], (DEPTH, D_MODEL, IN_WIDTH), jnp.float32) * D_MODEL ** -0.5
    conv_w = nrm(ks[3], (DEPTH, CONV_WIDTH, 3 * GDN_WIDTH), jnp.float32) * CONV_WIDTH ** -0.5
    w_pool = nrm(ks[4], (DEPTH, POOL_GROUPS, POOL_GROUP_DIM, POOL_GROUP_DIM), jnp.float32) * POOL_GROUP_DIM ** -0.5
    pool_scale = 1.0 + 0.1 * nrm(ks[5], (DEPTH, POOL_WIDTH), jnp.float32)
    a_log = jnp.log(jax.random.uniform(ks[6], (DEPTH, H), jnp.float32, 1.0, 16.0))
    dt = jnp.exp(jax.random.uniform(ks[7], (DEPTH, H), jnp.float32, math.log(1e-3), math.log(1e-1)))
    dt_bias = dt + jnp.log(-jnp.expm1(-dt))
    gdn_norm = 1.0 + 0.02 * nrm(ks[8], (DEPTH, Dh), jnp.float32)
    w_out = nrm(ks[9], (DEPTH, MIX_WIDTH, D_MODEL), jnp.float32) * MIX_WIDTH ** -0.5
    norm_mlp = 1.0 + 0.02 * nrm(ks[10], (DEPTH, D_MODEL), jnp.float32)
    w_up = nrm(ks[11], (DEPTH, D_MODEL, D_FF), jnp.float32) * D_MODEL ** -0.5
    w_down = nrm(ks[12], (DEPTH, D_FF, D_MODEL), jnp.float32) * D_FF ** -0.5
    norm_final = 1.0 + 0.02 * nrm(ks[13], (D_MODEL,), jnp.float32)
    return {"x": x, "norm_mix": norm_mix, "w_in": w_in, "conv_w": conv_w, "w_pool": w_pool,
            "pool_scale": pool_scale, "a_log": a_log, "dt_bias": dt_bias, "gdn_norm": gdn_norm,
            "w_out": w_out, "norm_mlp": norm_mlp, "w_up": w_up, "w_down": w_down,
            "norm_final": norm_final}


def reference(x, norm_mix, w_in, conv_w, w_pool, pool_scale, a_log, dt_bias, gdn_norm,
              w_out, norm_mlp, w_up, w_down, norm_final):
    P, G, H = POOL_WIDTH, GDN_WIDTH, GDN_HEADS
    h = x
    for l in range(DEPTH):
        u = rms_norm(h, norm_mix[l])
        proj = u @ w_in[l]
        pool_out = pool_mixer(proj[..., :P], w_pool[l], pool_scale[l])
        qkv = causal_conv_silu(proj[..., P:P + 3 * G], conv_w[l])
        z = proj[..., P + 3 * G:P + 4 * G]
        a = proj[..., P + 4 * G:P + 4 * G + H]
        b = proj[..., P + 4 * G + H:P + 4 * G + 2 * H]
        gdn_out = gated_delta_net(qkv[..., :G], qkv[..., G:2 * G], qkv[..., 2 * G:], z, a, b,
                                  a_log[l], dt_bias[l], gdn_norm[l])
        mix = jnp.concatenate([pool_out, gdn_out], axis=-1)
        h = h + mix @ w_out[l]
        u = rms_norm(h, norm_mlp[l])
        h = h + jnp.square(jax.nn.relu(u @ w_up[l])) @ w_down[l]
    return rms_norm(h, norm_final)
```

```python
import functools

import jax
import jax.numpy as jnp
from jax import lax
from jax.experimental import pallas as pl
from jax.experimental.pallas import tpu as pltpu

D_MODEL = 1024
CHUNK = 64
POOL_WIDTH = D_MODEL // 2
POOL_GROUPS = 4
POOL_GROUP_DIM = POOL_WIDTH // POOL_GROUPS
POOL_WINDOWS = (2, 4, 8, 16)
GDN_WIDTH = D_MODEL - POOL_WIDTH
GDN_HEADS = 4
GDN_HEAD_DIM = GDN_WIDTH // GDN_HEADS
CONV_WIDTH = 4
D_FF = 4 * D_MODEL
EPS = 1e-6

LANES = 128
SUBLANES = 8
MAIN_WIDTH = POOL_WIDTH + 4 * GDN_WIDTH
GATE_COL = MAIN_WIDTH
PROJ_WIDTH = MAIN_WIDTH + LANES
QKV_OFF = POOL_WIDTH
Z_OFF = POOL_WIDTH + 3 * GDN_WIDTH
CONV_HALO = SUBLANES
POOL_HALO = 16

TM_PROJ = 512
TS_MIX = 256
TM_MLP = 512
FF_CHUNK = 1024
VMEM_LIMIT = 56 * 1024 * 1024

F32 = jnp.float32
BF16 = jnp.bfloat16


def _rms_norm(x, w):
    ms = jnp.mean(x * x, axis=-1, keepdims=True)
    return x * lax.rsqrt(ms + EPS) * w


def _softplus(x):
    return jnp.maximum(x, 0.0) + jnp.log1p(jnp.exp(-jnp.abs(x)))


def _sigmoid(x):
    return 1.0 / (1.0 + jnp.exp(-x))


def _bdot(a, b):
    return jnp.einsum('cik,ckj->cij', a.astype(BF16), b.astype(BF16),
                      preferred_element_type=F32)


def _bdot_nt(a, b):
    return jnp.einsum('cid,cjd->cij', a.astype(BF16), b.astype(BF16),
                      preferred_element_type=F32)


def _inproj_kernel(h_ref, nw_ref, w_ref, wgt_ref, proj_ref, gt_ref):
    u = _rms_norm(h_ref[0], nw_ref[...]).astype(BF16)
    proj_ref[0] = jnp.dot(u, w_ref[...], preferred_element_type=F32)
    gt = lax.dot_general(wgt_ref[...], u, (((1,), (1,)), ((), ())),
                         preferred_element_type=F32)
    for c in range(TM_PROJ // CHUNK):
        gt_ref[0, c] = gt[:, c * CHUNK:(c + 1) * CHUNK]


def _inproj(h, nw, w_main, w_gate_t):
    B, S, D = h.shape
    n_tiles = S // TM_PROJ
    cpt = TM_PROJ // CHUNK
    const = dict(pipeline_mode=pl.Buffered(1))
    return pl.pallas_call(
        _inproj_kernel,
        grid=(B, n_tiles),
        in_specs=[
            pl.BlockSpec((1, TM_PROJ, D), lambda b, i: (b, i, 0)),
            pl.BlockSpec((1, D), lambda b, i: (0, 0), **const),
            pl.BlockSpec((D, PROJ_WIDTH), lambda b, i: (0, 0), **const),
            pl.BlockSpec((2 * GDN_HEADS, D), lambda b, i: (0, 0), **const),
        ],
        out_specs=[
            pl.BlockSpec((1, TM_PROJ, PROJ_WIDTH), lambda b, i: (b, i, 0)),
            pl.BlockSpec((1, cpt, 2 * GDN_HEADS, CHUNK), lambda b, i: (b, i, 0, 0)),
        ],
        out_shape=[
            jax.ShapeDtypeStruct((B, S, PROJ_WIDTH), F32),
            jax.ShapeDtypeStruct((B, S // CHUNK, 2 * GDN_HEADS, CHUNK), F32),
        ],
        compiler_params=pltpu.CompilerParams(
            dimension_semantics=("arbitrary", "arbitrary"),
            vmem_limit_bytes=VMEM_LIMIT),
        name="inproj",
    )(h, nw, w_main, w_gate_t)


INV_BASE = 16


def _unit_lower_inverse(m):
    ri = lax.broadcasted_iota(jnp.int32, m.shape, 1)
    ci = lax.broadcasted_iota(jnp.int32, m.shape, 2)
    eye = (ri == ci).astype(F32)
    d = jnp.where(ri // INV_BASE == ci // INV_BASE, m, 0.0)
    t = eye - d
    p = _bdot(d, d)
    power = 2
    while power < INV_BASE:
        t = t + _bdot(t, p)
        power *= 2
        if power < INV_BASE:
            p = _bdot(p, p)
    b = INV_BASE
    while b < CHUNK:
        lower_left = ((ri // (2 * b) == ci // (2 * b))
                      & ((ri // b) % 2 == 1) & ((ci // b) % 2 == 0))
        c = jnp.where(lower_left, m, 0.0)
        t = t - _bdot(t, _bdot(c, t))
        b *= 2
    return t


def _mixer_kernel(proj_ref, gt_ref, h_ref, convw_ref, wpool_ref, pscale_ref,
                  alog_row_ref, dtb_row_ref, alog_col_ref, dtb_col_ref,
                  gnorm_ref, wout_ref, out_ref,
                  state_ref, conv_ref, pool_ref, mix_ref):
    ts = TS_MIX
    nc = ts // CHUNK
    i = pl.program_id(1)

    @pl.when(i == 0)
    def _():
        state_ref[...] = jnp.zeros_like(state_ref)
        conv_ref[0:CONV_HALO, :] = jnp.zeros((CONV_HALO, 3 * GDN_WIDTH), F32)
        pool_ref[0:POOL_HALO, :] = jnp.zeros((POOL_HALO, POOL_WIDTH), F32)

    conv_ref[CONV_HALO:CONV_HALO + ts, :] = proj_ref[0, :, QKV_OFF:QKV_OFF + 3 * GDN_WIDTH]
    y = conv_ref[CONV_HALO:CONV_HALO + ts, :] * convw_ref[CONV_WIDTH - 1:CONV_WIDTH, :]
    for j in range(1, CONV_WIDTH):
        y = y + (conv_ref[CONV_HALO - j:CONV_HALO - j + ts, :]
                 * convw_ref[CONV_WIDTH - 1 - j:CONV_WIDTH - j, :])
    qkv = y * _sigmoid(y)
    conv_ref[0:CONV_HALO, :] = conv_ref[ts:ts + CONV_HALO, :]

    pool_ref[POOL_HALO:POOL_HALO + ts, :] = proj_ref[0, :, 0:POOL_WIDTH]
    count = (lax.broadcasted_iota(jnp.int32, (ts, POOL_GROUP_DIM), 0)
             + (i * ts + 1)).astype(F32)
    for g, w in enumerate(POOL_WINDOWS):
        sl = slice(g * POOL_GROUP_DIM, (g + 1) * POOL_GROUP_DIM)
        x0 = pool_ref[POOL_HALO:POOL_HALO + ts, sl]
        acc = x0
        for j in range(1, w):
            acc = acc + pool_ref[POOL_HALO - j:POOL_HALO - j + ts, sl]
        pooled = acc / jnp.minimum(count, float(w)) - x0
        mixed = jnp.dot(pooled.astype(BF16), wpool_ref[g], preferred_element_type=F32)
        mix_ref[:, sl] = (mixed * pscale_ref[:, sl]).astype(BF16)
    pool_ref[0:POOL_HALO, :] = pool_ref[ts:ts + POOL_HALO, :]

    gate_c = proj_ref[0, :, GATE_COL:GATE_COL + LANES]
    g_c = -jnp.exp(alog_row_ref[...]) * _softplus(gate_c + dtb_row_ref[...])
    beta_c = _sigmoid(gate_c)
    r_i = lax.broadcasted_iota(jnp.int32, (ts, ts), 0)
    c_i = lax.broadcasted_iota(jnp.int32, (ts, ts), 1)
    blk_lower = ((r_i // CHUNK == c_i // CHUNK) & (c_i <= r_i)).astype(F32)
    gcum_c = jnp.dot(blk_lower, g_c, preferred_element_type=F32,
                     precision=lax.Precision.HIGHEST)

    gate_r = gt_ref[0].reshape(nc * 2 * GDN_HEADS, CHUNK)
    alog_col = jnp.concatenate([alog_col_ref[...]] * nc, axis=0)
    dtb_col = jnp.concatenate([dtb_col_ref[...]] * nc, axis=0)
    g_r = -jnp.exp(alog_col) * _softplus(gate_r + dtb_col)
    beta_r = _sigmoid(gate_r).reshape(nc, 2 * GDN_HEADS, CHUNK)
    u_i = lax.broadcasted_iota(jnp.int32, (CHUNK, CHUNK), 0)
    v_i = lax.broadcasted_iota(jnp.int32, (CHUNK, CHUNK), 1)
    upper = (u_i <= v_i).astype(F32)
    gcum_r = jnp.dot(g_r, upper, preferred_element_type=F32,
                     precision=lax.Precision.HIGHEST).reshape(nc, 2 * GDN_HEADS, CHUNK)

    ri = lax.broadcasted_iota(jnp.int32, (nc, CHUNK, CHUNK), 1)
    ci = lax.broadcasted_iota(jnp.int32, (nc, CHUNK, CHUNK), 2)
    incl = ri >= ci
    strict = ri > ci

    for hd in range(GDN_HEADS):
        hs = slice(hd * GDN_HEAD_DIM, (hd + 1) * GDN_HEAD_DIM)
        q = qkv[:, hd * GDN_HEAD_DIM:(hd + 1) * GDN_HEAD_DIM]
        k = qkv[:, GDN_WIDTH + hd * GDN_HEAD_DIM:GDN_WIDTH + (hd + 1) * GDN_HEAD_DIM]
        v = qkv[:, 2 * GDN_WIDTH + hd * GDN_HEAD_DIM:2 * GDN_WIDTH + (hd + 1) * GDN_HEAD_DIM]
        q = q.reshape(nc, CHUNK, GDN_HEAD_DIM)
        k = k.reshape(nc, CHUNK, GDN_HEAD_DIM)
        v = v.reshape(nc, CHUNK, GDN_HEAD_DIM)
        qh = q * (lax.rsqrt(jnp.sum(q * q, axis=-1, keepdims=True) + EPS)
                  * (GDN_HEAD_DIM ** -0.5))
        kh = k * lax.rsqrt(jnp.sum(k * k, axis=-1, keepdims=True) + EPS)

        gc = gcum_c[:, hd:hd + 1].reshape(nc, CHUNK, 1)
        bc = beta_c[:, GDN_HEADS + hd:GDN_HEADS + hd + 1].reshape(nc, CHUNK, 1)
        gr = gcum_r[:, hd:hd + 1, :]
        g_last = gc[:, CHUNK - 1:CHUNK, :]
        eg = jnp.exp(gc)

        decay_incl = jnp.exp(jnp.where(incl, gc - gr, -jnp.inf))
        decay_strict = jnp.where(strict, decay_incl, 0.0)

        kh_b = kh.astype(BF16)
        m = _bdot_nt(kh_b, kh_b) * bc * decay_strict
        t_inv = _unit_lower_inverse(m)
        rhs = jnp.concatenate([v * bc, kh * (bc * eg)], axis=-1)
        sol = _bdot(t_inv, rhs)
        value = sol[:, :, :GDN_HEAD_DIM]
        k_cumdecay = sol[:, :, GDN_HEAD_DIM:]
        attn = _bdot_nt(qh, kh_b) * decay_incl
        q_dec = qh * eg
        k_dec = kh * jnp.exp(g_last - gc)

        state = state_ref[hd]
        outs = []
        for c in range(nc):
            lhs = jnp.concatenate([k_cumdecay[c], q_dec[c]], axis=0).astype(BF16)
            r = jnp.dot(lhs, state.astype(BF16), preferred_element_type=F32)
            v_new = value[c] - r[:CHUNK]
            v_new_b = v_new.astype(BF16)
            o = r[CHUNK:] + jnp.dot(attn[c].astype(BF16), v_new_b,
                                    preferred_element_type=F32)
            state = state * jnp.exp(g_last[c]) + lax.dot_general(
                k_dec[c].astype(BF16), v_new_b, (((0,), (0,)), ((), ())),
                preferred_element_type=F32)
            outs.append(o)
        state_ref[hd] = state
        o = jnp.concatenate(outs, axis=0)
        o = o * lax.rsqrt(jnp.mean(o * o, axis=-1, keepdims=True) + EPS) * gnorm_ref[...]
        z = proj_ref[0, :, Z_OFF + hd * GDN_HEAD_DIM:Z_OFF + (hd + 1) * GDN_HEAD_DIM]
        o = o * (z * _sigmoid(z))
        mix_ref[:, POOL_WIDTH + hd * GDN_HEAD_DIM:POOL_WIDTH + (hd + 1) * GDN_HEAD_DIM] = (
            o.astype(BF16))

    out_ref[0] = h_ref[0] + jnp.dot(mix_ref[...], wout_ref[...], preferred_element_type=F32)


def _mixer(proj, gate_t, h, conv_w, w_pool, pool_scale, alog_row, dtb_row, alog_col, dtb_col,
           gdn_norm, w_out):
    B, S, D = h.shape
    ts = TS_MIX
    nc = ts // CHUNK
    const = dict(pipeline_mode=pl.Buffered(1))
    zero2 = lambda b, i: (0, 0)
    return pl.pallas_call(
        _mixer_kernel,
        grid=(B, S // ts),
        in_specs=[
            pl.BlockSpec((1, ts, PROJ_WIDTH), lambda b, i: (b, i, 0)),
            pl.BlockSpec((1, nc, 2 * GDN_HEADS, CHUNK), lambda b, i: (b, i, 0, 0)),
            pl.BlockSpec((1, ts, D), lambda b, i: (b, i, 0)),
            pl.BlockSpec((CONV_WIDTH, 3 * GDN_WIDTH), zero2, **const),
            pl.BlockSpec((POOL_GROUPS, POOL_GROUP_DIM, POOL_GROUP_DIM),
                         lambda b, i: (0, 0, 0), **const),
            pl.BlockSpec((1, POOL_WIDTH), zero2, **const),
            pl.BlockSpec((1, LANES), zero2, **const),
            pl.BlockSpec((1, LANES), zero2, **const),
            pl.BlockSpec((2 * GDN_HEADS, 1), zero2, **const),
            pl.BlockSpec((2 * GDN_HEADS, 1), zero2, **const),
            pl.BlockSpec((1, GDN_HEAD_DIM), zero2, **const),
            pl.BlockSpec((D, D), zero2, **const),
        ],
        out_specs=pl.BlockSpec((1, ts, D), lambda b, i: (b, i, 0)),
        out_shape=jax.ShapeDtypeStruct((B, S, D), F32),
        scratch_shapes=[
            pltpu.VMEM((GDN_HEADS, GDN_HEAD_DIM, GDN_HEAD_DIM), F32),
            pltpu.VMEM((ts + CONV_HALO, 3 * GDN_WIDTH), F32),
            pltpu.VMEM((ts + POOL_HALO, POOL_WIDTH), F32),
            pltpu.VMEM((ts, D), BF16),
        ],
        compiler_params=pltpu.CompilerParams(
            dimension_semantics=("arbitrary", "arbitrary"),
            vmem_limit_bytes=VMEM_LIMIT),
        name="mixer",
    )(proj, gate_t, h, conv_w, w_pool, pool_scale, alog_row, dtb_row, alog_col, dtb_col,
      gdn_norm, w_out)


def _mlp_kernel(h_ref, nw_ref, wup_ref, wdown_ref, nf_ref, out_ref, *, final_norm):
    x = h_ref[0]
    u = _rms_norm(x, nw_ref[...]).astype(BF16)
    acc = x
    for j in range(D_FF // FF_CHUNK):
        cols = slice(j * FF_CHUNK, (j + 1) * FF_CHUNK)
        hid = jnp.dot(u, wup_ref[:, cols], preferred_element_type=F32)
        hid = jnp.square(jnp.maximum(hid, 0.0)).astype(BF16)
        acc = acc + jnp.dot(hid, wdown_ref[cols, :], preferred_element_type=F32)
    if final_norm:
        acc = _rms_norm(acc, nf_ref[...])
    out_ref[0] = acc


def _mlp(h, nw, w_up, w_down, nf, final_norm):
    B, S, D = h.shape
    const = dict(pipeline_mode=pl.Buffered(1))
    zero2 = lambda b, i: (0, 0)
    return pl.pallas_call(
        functools.partial(_mlp_kernel, final_norm=final_norm),
        grid=(B, S // TM_MLP),
        in_specs=[
            pl.BlockSpec((1, TM_MLP, D), lambda b, i: (b, i, 0)),
            pl.BlockSpec((1, D), zero2, **const),
            pl.BlockSpec((D, D_FF), zero2, **const),
            pl.BlockSpec((D_FF, D), zero2, **const),
            pl.BlockSpec((1, D), zero2, **const),
        ],
        out_specs=pl.BlockSpec((1, TM_MLP, D), lambda b, i: (b, i, 0)),
        out_shape=jax.ShapeDtypeStruct((B, S, D), F32),
        compiler_params=pltpu.CompilerParams(
            dimension_semantics=("arbitrary", "arbitrary"),
            vmem_limit_bytes=VMEM_LIMIT),
        name="mlp_final" if final_norm else "mlp",
    )(h, nw, w_up, w_down, nf)


def kernel(x, norm_mix, w_in, conv_w, w_pool, pool_scale, a_log, dt_bias, gdn_norm, w_out,
           norm_mlp, w_up, w_down, norm_final):
    depth = w_in.shape[0]
    H = GDN_HEADS
    h = x
    for l in range(depth):
        w_main = jnp.pad(w_in[l][:, :MAIN_WIDTH + 2 * H],
                         ((0, 0), (0, LANES - 2 * H))).astype(BF16)
        w_gate_t = w_in[l][:, MAIN_WIDTH:MAIN_WIDTH + 2 * H].T.astype(BF16)
        pad_row = lambda p: jnp.pad(p, (0, LANES - H)).reshape(1, LANES)
        pad_col = lambda p: jnp.pad(p, (0, H)).reshape(2 * H, 1)

        proj, gate_t = _inproj(h, norm_mix[l].reshape(1, D_MODEL), w_main, w_gate_t)
        h = _mixer(proj, gate_t, h, conv_w[l], w_pool[l].astype(BF16),
                   pool_scale[l].reshape(1, POOL_WIDTH),
                   pad_row(a_log[l]), pad_row(dt_bias[l]), pad_col(a_log[l]), pad_col(dt_bias[l]),
                   gdn_norm[l].reshape(1, GDN_HEAD_DIM), w_out[l].astype(BF16))
        h = _mlp(h, norm_mlp[l].reshape(1, D_MODEL), w_up[l].astype(BF16),
                 w_down[l].astype(BF16), norm_final.reshape(1, D_MODEL),
                 final_norm=(l == depth - 1))
    return h
```

```python
import functools

import jax
import jax.numpy as jnp
from jax import lax
from jax.experimental import pallas as pl
from jax.experimental.pallas import tpu as pltpu

D_MODEL = 1024
CHUNK = 64
POOL_WIDTH = D_MODEL // 2
POOL_GROUPS = 4
POOL_GROUP_DIM = POOL_WIDTH // POOL_GROUPS
POOL_WINDOWS = (2, 4, 8, 16)
GDN_WIDTH = D_MODEL - POOL_WIDTH
GDN_HEADS = 4
GDN_HEAD_DIM = GDN_WIDTH // GDN_HEADS
CONV_WIDTH = 4
D_FF = 4 * D_MODEL
EPS = 1e-6

LANES = 128
SUBLANES = 8
MAIN_WIDTH = POOL_WIDTH + 4 * GDN_WIDTH
GATE_COL = MAIN_WIDTH
PROJ_WIDTH = MAIN_WIDTH + LANES
QKV_OFF = POOL_WIDTH
Z_OFF = POOL_WIDTH + 3 * GDN_WIDTH
CONV_HALO = SUBLANES
POOL_HALO = 16

TM_PROJ = 512
TS_MIX = 256
TM_MLP = 512
FF_CHUNK = 1024
VMEM_LIMIT = 56 * 1024 * 1024

F32 = jnp.float32
BF16 = jnp.bfloat16


def _rms_norm(x, w):
    ms = jnp.mean(x * x, axis=-1, keepdims=True)
    return x * lax.rsqrt(ms + EPS) * w


def _softplus(x):
    return jnp.maximum(x, 0.0) + jnp.log1p(jnp.exp(-jnp.abs(x)))


def _sigmoid(x):
    return 1.0 / (1.0 + jnp.exp(-x))


def _bdot(a, b):
    return jnp.einsum('cik,ckj->cij', a.astype(BF16), b.astype(BF16),
                      preferred_element_type=F32)


def _bdot_nt(a, b):
    return jnp.einsum('cid,cjd->cij', a.astype(BF16), b.astype(BF16),
                      preferred_element_type=F32)


def _inproj_kernel(h_ref, nw_ref, w_ref, wgt_ref, proj_ref, gt_ref):
    u = _rms_norm(h_ref[0], nw_ref[...]).astype(BF16)
    proj_ref[0] = jnp.dot(u, w_ref[...], preferred_element_type=F32)
    gt = lax.dot_general(wgt_ref[...], u, (((1,), (1,)), ((), ())),
                         preferred_element_type=F32)
    for c in range(TM_PROJ // CHUNK):
        gt_ref[0, c] = gt[:, c * CHUNK:(c + 1) * CHUNK]


def _inproj(h, nw, w_main, w_gate_t):
    B, S, D = h.shape
    n_tiles = S // TM_PROJ
    cpt = TM_PROJ // CHUNK
    const = dict(pipeline_mode=pl.Buffered(1))
    return pl.pallas_call(
        _inproj_kernel,
        grid=(B, n_tiles),
        in_specs=[
            pl.BlockSpec((1, TM_PROJ, D), lambda b, i: (b, i, 0)),
            pl.BlockSpec((1, D), lambda b, i: (0, 0), **const),
            pl.BlockSpec((D, PROJ_WIDTH), lambda b, i: (0, 0), **const),
            pl.BlockSpec((2 * GDN_HEADS, D), lambda b, i: (0, 0), **const),
        ],
        out_specs=[
            pl.BlockSpec((1, TM_PROJ, PROJ_WIDTH), lambda b, i: (b, i, 0)),
            pl.BlockSpec((1, cpt, 2 * GDN_HEADS, CHUNK), lambda b, i: (b, i, 0, 0)),
        ],
        out_shape=[
            jax.ShapeDtypeStruct((B, S, PROJ_WIDTH), F32),
            jax.ShapeDtypeStruct((B, S // CHUNK, 2 * GDN_HEADS, CHUNK), F32),
        ],
        compiler_params=pltpu.CompilerParams(
            dimension_semantics=("arbitrary", "arbitrary"),
            vmem_limit_bytes=VMEM_LIMIT),
        name="inproj",
    )(h, nw, w_main, w_gate_t)


INV_BASE = 16


def _unit_lower_inverse(m):
    ri = lax.broadcasted_iota(jnp.int32, m.shape, 1)
    ci = lax.broadcasted_iota(jnp.int32, m.shape, 2)
    eye = (ri == ci).astype(F32)
    d = jnp.where(ri // INV_BASE == ci // INV_BASE, m, 0.0)
    t = eye - d
    p = _bdot(d, d)
    power = 2
    while power < INV_BASE:
        t = t + _bdot(t, p)
        power *= 2
        if power < INV_BASE:
            p = _bdot(p, p)
    b = INV_BASE
    while b < CHUNK:
        lower_left = ((ri // (2 * b) == ci // (2 * b))
                      & ((ri // b) % 2 == 1) & ((ci // b) % 2 == 0))
        c = jnp.where(lower_left, m, 0.0)
        t = t - _bdot(t, _bdot(c, t))
        b *= 2
    return t


def _mixer_kernel(proj_ref, gt_ref, h_ref, convw_ref, wpool_ref, pscale_ref,
                  alog_row_ref, dtb_row_ref, alog_col_ref, dtb_col_ref,
                  gnorm_ref, wout_ref, out_ref,
                  state_ref, conv_ref, pool_ref, mix_ref):
    ts = TS_MIX
    nc = ts // CHUNK
    i = pl.program_id(1)

    @pl.when(i == 0)
    def _():
        state_ref[...] = jnp.zeros_like(state_ref)
        conv_ref[0:CONV_HALO, :] = jnp.zeros((CONV_HALO, 3 * GDN_WIDTH), F32)
        pool_ref[0:POOL_HALO, :] = jnp.zeros((POOL_HALO, POOL_WIDTH), F32)

    conv_ref[CONV_HALO:CONV_HALO + ts, :] = proj_ref[0, :, QKV_OFF:QKV_OFF + 3 * GDN_WIDTH]
    y = conv_ref[CONV_HALO:CONV_HALO + ts, :] * convw_ref[CONV_WIDTH - 1:CONV_WIDTH, :]
    for j in range(1, CONV_WIDTH):
        y = y + (conv_ref[CONV_HALO - j:CONV_HALO - j + ts, :]
                 * convw_ref[CONV_WIDTH - 1 - j:CONV_WIDTH - j, :])
    qkv = y * _sigmoid(y)
    conv_ref[0:CONV_HALO, :] = conv_ref[ts:ts + CONV_HALO, :]

    pool_ref[POOL_HALO:POOL_HALO + ts, :] = proj_ref[0, :, 0:POOL_WIDTH]
    count = (lax.broadcasted_iota(jnp.int32, (ts, POOL_GROUP_DIM), 0)
             + (i * ts + 1)).astype(F32)
    for g, w in enumerate(POOL_WINDOWS):
        sl = slice(g * POOL_GROUP_DIM, (g + 1) * POOL_GROUP_DIM)
        x0 = pool_ref[POOL_HALO:POOL_HALO + ts, sl]
        acc = x0
        for j in range(1, w):
            acc = acc + pool_ref[POOL_HALO - j:POOL_HALO - j + ts, sl]
        pooled = acc / jnp.minimum(count, float(w)) - x0
        mixed = jnp.dot(pooled.astype(BF16), wpool_ref[g], preferred_element_type=F32)
        mix_ref[:, sl] = (mixed * pscale_ref[:, sl]).astype(BF16)
    pool_ref[0:POOL_HALO, :] = pool_ref[ts:ts + POOL_HALO, :]

    gate_c = proj_ref[0, :, GATE_COL:GATE_COL + LANES]
    g_c = -jnp.exp(alog_row_ref[...]) * _softplus(gate_c + dtb_row_ref[...])
    beta_c = _sigmoid(gate_c)
    r_i = lax.broadcasted_iota(jnp.int32, (ts, ts), 0)
    c_i = lax.broadcasted_iota(jnp.int32, (ts, ts), 1)
    blk_lower = ((r_i // CHUNK == c_i // CHUNK) & (c_i <= r_i)).astype(F32)
    gcum_c = jnp.dot(blk_lower, g_c, preferred_element_type=F32,
                     precision=lax.Precision.HIGHEST)

    gate_r = gt_ref[0].reshape(nc * 2 * GDN_HEADS, CHUNK)
    alog_col = jnp.concatenate([alog_col_ref[...]] * nc, axis=0)
    dtb_col = jnp.concatenate([dtb_col_ref[...]] * nc, axis=0)
    g_r = -jnp.exp(alog_col) * _softplus(gate_r + dtb_col)
    beta_r = _sigmoid(gate_r).reshape(nc, 2 * GDN_HEADS, CHUNK)
    u_i = lax.broadcasted_iota(jnp.int32, (CHUNK, CHUNK), 0)
    v_i = lax.broadcasted_iota(jnp.int32, (CHUNK, CHUNK), 1)
    upper = (u_i <= v_i).astype(F32)
    gcum_r = jnp.dot(g_r, upper, preferred_element_type=F32,
                     precision=lax.Precision.HIGHEST).reshape(nc, 2 * GDN_HEADS, CHUNK)

    H = GDN_HEADS
    dh = GDN_HEAD_DIM

    def heads(x2d):
        return jnp.stack([x2d[:, hd * dh:(hd + 1) * dh] for hd in range(H)],
                         axis=0).reshape(H * nc, CHUNK, dh)

    def head_cols(x2d, off):
        return jnp.stack([x2d[:, off + hd:off + hd + 1] for hd in range(H)],
                         axis=0).reshape(H * nc, CHUNK, 1)

    def head_rows(x3d, off):
        return jnp.stack([x3d[:, off + hd:off + hd + 1, :] for hd in range(H)],
                         axis=0).reshape(H * nc, 1, CHUNK)

    q = heads(qkv[:, 0:GDN_WIDTH])
    k = heads(qkv[:, GDN_WIDTH:2 * GDN_WIDTH])
    v = heads(qkv[:, 2 * GDN_WIDTH:3 * GDN_WIDTH])
    qh = q * (lax.rsqrt(jnp.sum(q * q, axis=-1, keepdims=True) + EPS) * (dh ** -0.5))
    kh = k * lax.rsqrt(jnp.sum(k * k, axis=-1, keepdims=True) + EPS)

    gc = head_cols(gcum_c, 0)
    bc = head_cols(beta_c, H)
    gr = head_rows(gcum_r, 0)
    g_last = gc[:, CHUNK - 1:CHUNK, :]
    eg = jnp.exp(gc)

    ri = lax.broadcasted_iota(jnp.int32, (H * nc, CHUNK, CHUNK), 1)
    ci = lax.broadcasted_iota(jnp.int32, (H * nc, CHUNK, CHUNK), 2)
    decay_incl = jnp.exp(jnp.where(ri >= ci, gc - gr, -jnp.inf))
    decay_strict = jnp.where(ri > ci, decay_incl, 0.0)

    kh_b = kh.astype(BF16)
    m = _bdot_nt(kh_b, kh_b) * bc * decay_strict
    t_inv = _unit_lower_inverse(m)
    rhs = jnp.concatenate([v * bc, kh * (bc * eg)], axis=-1)
    sol = _bdot(t_inv, rhs)
    per_chunk = lambda x: x.reshape((H, nc) + x.shape[1:])
    value = per_chunk(sol[:, :, :dh])
    k_cumdecay = per_chunk(sol[:, :, dh:])
    attn = per_chunk(_bdot_nt(qh, kh_b) * decay_incl)
    q_dec = per_chunk(qh * eg)
    k_dec = per_chunk(kh * jnp.exp(g_last - gc))
    state_decay = per_chunk(jnp.exp(g_last))

    state = state_ref[...]
    outs = []
    for c in range(nc):
        lhs = jnp.concatenate([k_cumdecay[:, c], q_dec[:, c]], axis=1)
        r = _bdot(lhs, state)
        v_new = (value[:, c] - r[:, :CHUNK]).astype(BF16)
        outs.append(r[:, CHUNK:] + _bdot(attn[:, c], v_new))
        state = state * state_decay[:, c] + jnp.einsum(
            'hcd,hce->hde', k_dec[:, c].astype(BF16), v_new, preferred_element_type=F32)
    state_ref[...] = state

    o = jnp.concatenate(outs, axis=1)
    o = o * lax.rsqrt(jnp.mean(o * o, axis=-1, keepdims=True) + EPS) * gnorm_ref[...]
    for hd in range(H):
        z = proj_ref[0, :, Z_OFF + hd * dh:Z_OFF + (hd + 1) * dh]
        mix_ref[:, POOL_WIDTH + hd * dh:POOL_WIDTH + (hd + 1) * dh] = (
            o[hd] * (z * _sigmoid(z))).astype(BF16)

    out_ref[0] = h_ref[0] + jnp.dot(mix_ref[...], wout_ref[...], preferred_element_type=F32)


def _mixer(proj, gate_t, h, conv_w, w_pool, pool_scale, alog_row, dtb_row, alog_col, dtb_col,
           gdn_norm, w_out):
    B, S, D = h.shape
    ts = TS_MIX
    nc = ts // CHUNK
    const = dict(pipeline_mode=pl.Buffered(1))
    zero2 = lambda b, i: (0, 0)
    return pl.pallas_call(
        _mixer_kernel,
        grid=(B, S // ts),
        in_specs=[
            pl.BlockSpec((1, ts, PROJ_WIDTH), lambda b, i: (b, i, 0)),
            pl.BlockSpec((1, nc, 2 * GDN_HEADS, CHUNK), lambda b, i: (b, i, 0, 0)),
            pl.BlockSpec((1, ts, D), lambda b, i: (b, i, 0)),
            pl.BlockSpec((CONV_WIDTH, 3 * GDN_WIDTH), zero2, **const),
            pl.BlockSpec((POOL_GROUPS, POOL_GROUP_DIM, POOL_GROUP_DIM),
                         lambda b, i: (0, 0, 0), **const),
            pl.BlockSpec((1, POOL_WIDTH), zero2, **const),
            pl.BlockSpec((1, LANES), zero2, **const),
            pl.BlockSpec((1, LANES), zero2, **const),
            pl.BlockSpec((2 * GDN_HEADS, 1), zero2, **const),
            pl.BlockSpec((2 * GDN_HEADS, 1), zero2, **const),
            pl.BlockSpec((1, GDN_HEAD_DIM), zero2, **const),
            pl.BlockSpec((D, D), zero2, **const),
        ],
        out_specs=pl.BlockSpec((1, ts, D), lambda b, i: (b, i, 0)),
        out_shape=jax.ShapeDtypeStruct((B, S, D), F32),
        scratch_shapes=[
            pltpu.VMEM((GDN_HEADS, GDN_HEAD_DIM, GDN_HEAD_DIM), F32),
            pltpu.VMEM((ts + CONV_HALO, 3 * GDN_WIDTH), F32),
            pltpu.VMEM((ts + POOL_HALO, POOL_WIDTH), F32),
            pltpu.VMEM((ts, D), BF16),
        ],
        compiler_params=pltpu.CompilerParams(
            dimension_semantics=("arbitrary", "arbitrary"),
            vmem_limit_bytes=VMEM_LIMIT),
        name="mixer",
    )(proj, gate_t, h, conv_w, w_pool, pool_scale, alog_row, dtb_row, alog_col, dtb_col,
      gdn_norm, w_out)


def _mlp_kernel(h_ref, nw_ref, wup_ref, wdown_ref, nf_ref, out_ref, *, final_norm):
    x = h_ref[0]
    u = _rms_norm(x, nw_ref[...]).astype(BF16)
    acc = x
    for j in range(D_FF // FF_CHUNK):
        cols = slice(j * FF_CHUNK, (j + 1) * FF_CHUNK)
        hid = jnp.dot(u, wup_ref[:, cols], preferred_element_type=F32)
        hid = jnp.square(jnp.maximum(hid, 0.0)).astype(BF16)
        acc = acc + jnp.dot(hid, wdown_ref[cols, :], preferred_element_type=F32)
    if final_norm:
        acc = _rms_norm(acc, nf_ref[...])
    out_ref[0] = acc


def _mlp(h, nw, w_up, w_down, nf, final_norm):
    B, S, D = h.shape
    const = dict(pipeline_mode=pl.Buffered(1))
    zero2 = lambda b, i: (0, 0)
    return pl.pallas_call(
        functools.partial(_mlp_kernel, final_norm=final_norm),
        grid=(B, S // TM_MLP),
        in_specs=[
            pl.BlockSpec((1, TM_MLP, D), lambda b, i: (b, i, 0)),
            pl.BlockSpec((1, D), zero2, **const),
            pl.BlockSpec((D, D_FF), zero2, **const),
            pl.BlockSpec((D_FF, D), zero2, **const),
            pl.BlockSpec((1, D), zero2, **const),
        ],
        out_specs=pl.BlockSpec((1, TM_MLP, D), lambda b, i: (b, i, 0)),
        out_shape=jax.ShapeDtypeStruct((B, S, D), F32),
        compiler_params=pltpu.CompilerParams(
            dimension_semantics=("arbitrary", "arbitrary"),
            vmem_limit_bytes=VMEM_LIMIT),
        name="mlp_final" if final_norm else "mlp",
    )(h, nw, w_up, w_down, nf)


def kernel(x, norm_mix, w_in, conv_w, w_pool, pool_scale, a_log, dt_bias, gdn_norm, w_out,
           norm_mlp, w_up, w_down, norm_final):
    depth = w_in.shape[0]
    H = GDN_HEADS
    h = x
    for l in range(depth):
        w_main = jnp.pad(w_in[l][:, :MAIN_WIDTH + 2 * H],
                         ((0, 0), (0, LANES - 2 * H))).astype(BF16)
        w_gate_t = w_in[l][:, MAIN_WIDTH:MAIN_WIDTH + 2 * H].T.astype(BF16)
        pad_row = lambda p: jnp.pad(p, (0, LANES - H)).reshape(1, LANES)
        pad_col = lambda p: jnp.pad(p, (0, H)).reshape(2 * H, 1)

        proj, gate_t = _inproj(h, norm_mix[l].reshape(1, D_MODEL), w_main, w_gate_t)
        h = _mixer(proj, gate_t, h, conv_w[l], w_pool[l].astype(BF16),
                   pool_scale[l].reshape(1, POOL_WIDTH),
                   pad_row(a_log[l]), pad_row(dt_bias[l]), pad_col(a_log[l]), pad_col(dt_bias[l]),
                   gdn_norm[l].reshape(1, GDN_HEAD_DIM), w_out[l].astype(BF16))
        h = _mlp(h, norm_mlp[l].reshape(1, D_MODEL), w_up[l].astype(BF16),
                 w_down[l].astype(BF16), norm_final.reshape(1, D_MODEL),
                 final_norm=(l == depth - 1))
    return h
```

```python
import functools

import jax
import jax.numpy as jnp
from jax import lax
from jax.experimental import pallas as pl
from jax.experimental.pallas import tpu as pltpu

D_MODEL = 1024
CHUNK = 64
POOL_WIDTH = D_MODEL // 2
POOL_GROUPS = 4
POOL_GROUP_DIM = POOL_WIDTH // POOL_GROUPS
POOL_WINDOWS = (2, 4, 8, 16)
GDN_WIDTH = D_MODEL - POOL_WIDTH
GDN_HEADS = 4
GDN_HEAD_DIM = GDN_WIDTH // GDN_HEADS
CONV_WIDTH = 4
D_FF = 4 * D_MODEL
EPS = 1e-6

LANES = 128
SUBLANES = 8
MAIN_WIDTH = POOL_WIDTH + 4 * GDN_WIDTH
GATE_COL = MAIN_WIDTH
PROJ_WIDTH = MAIN_WIDTH + LANES
QKV_OFF = POOL_WIDTH
Z_OFF = POOL_WIDTH + 3 * GDN_WIDTH
CONV_HALO = SUBLANES
POOL_HALO = 16
INV_BASE = 16

TS = 256
NC = TS // CHUNK
FF_CHUNK = 512
PROJ_CHUNK = 256
VMEM_LIMIT = 58 * 1024 * 1024

F32 = jnp.float32
BF16 = jnp.bfloat16


def _rms_norm(x, w):
    ms = jnp.mean(x * x, axis=-1, keepdims=True)
    return x * lax.rsqrt(ms + EPS) * w


def _softplus(x):
    return jnp.maximum(x, 0.0) + jnp.log1p(jnp.exp(-jnp.abs(x)))


def _sigmoid(x):
    return 1.0 / (1.0 + jnp.exp(-x))


def _bdot(a, b):
    return jnp.einsum('cik,ckj->cij', a.astype(BF16), b.astype(BF16),
                      preferred_element_type=F32)


def _bdot_nt(a, b):
    return jnp.einsum('cid,cjd->cij', a.astype(BF16), b.astype(BF16),
                      preferred_element_type=F32)


def _unit_lower_inverse(m, fill):
    ri = lax.broadcasted_iota(jnp.int32, m.shape, 1)
    ci = lax.broadcasted_iota(jnp.int32, m.shape, 2)
    eye = (ri == ci).astype(F32)
    d = jnp.where(ri // INV_BASE == ci // INV_BASE, m, 0.0)
    t = eye - d
    p = _bdot(d, d)
    fill()
    power = 2
    while power < INV_BASE:
        t = t + _bdot(t, p)
        power *= 2
        if power < INV_BASE:
            p = _bdot(p, p)
        fill()
    b = INV_BASE
    while b < CHUNK:
        lower_left = ((ri // (2 * b) == ci // (2 * b))
                      & ((ri // b) % 2 == 1) & ((ci // b) % 2 == 0))
        c = jnp.where(lower_left, m, 0.0)
        ct = _bdot(c, t)
        fill()
        t = t - _bdot(t, ct)
        b *= 2
        fill()
    return t


def _inproj_pieces(x_ref, nw_ref, w_ref, wgt_ref, proj_dst, gt_dst, h_dst, u_ref):
    def prepare():
        x = x_ref[...]
        h_dst[...] = x
        u_ref[...] = _rms_norm(x, nw_ref[...]).astype(BF16)

    def project(cols):
        def piece():
            proj_dst[:, cols] = jnp.dot(u_ref[...], w_ref[:, cols], preferred_element_type=F32)
        return piece

    def gates():
        gt = lax.dot_general(wgt_ref[...], u_ref[...], (((1,), (1,)), ((), ())),
                             preferred_element_type=F32)
        for c in range(NC):
            gt_dst[c] = gt[:, c * CHUNK:(c + 1) * CHUNK]

    starts = range(0, PROJ_WIDTH, PROJ_CHUNK)
    return ([prepare] + [project(slice(c0, min(c0 + PROJ_CHUNK, PROJ_WIDTH))) for c0 in starts]
            + [gates])


def _mixer_stage(proj_ref, gt_ref, h_ref, tile_in_row, convw_ref, wpool_ref, pscale_ref,
                 alog_row_ref, dtb_row_ref, alog_col_ref, dtb_col_ref, gnorm_ref, wout_ref,
                 hmid_dst, state_ref, conv_ref, pool_ref, mix_ref, fill):
    conv_ref[CONV_HALO:CONV_HALO + TS, :] = proj_ref[:, QKV_OFF:QKV_OFF + 3 * GDN_WIDTH]
    qkv_blocks = []
    half = TS // 2
    for blk in range(3):
        cs = slice(blk * GDN_WIDTH, (blk + 1) * GDN_WIDTH)
        parts = []
        for r0 in (CONV_HALO, CONV_HALO + half):
            y = conv_ref[r0:r0 + half, cs] * convw_ref[CONV_WIDTH - 1:CONV_WIDTH, cs]
            for j in range(1, CONV_WIDTH):
                y = y + (conv_ref[r0 - j:r0 - j + half, cs]
                         * convw_ref[CONV_WIDTH - 1 - j:CONV_WIDTH - j, cs])
            parts.append(y * _sigmoid(y))
            fill()
        qkv_blocks.append(jnp.concatenate(parts, axis=0))
    conv_ref[0:CONV_HALO, :] = conv_ref[TS:TS + CONV_HALO, :]

    pool_ref[POOL_HALO:POOL_HALO + TS, :] = proj_ref[:, 0:POOL_WIDTH]
    count = (lax.broadcasted_iota(jnp.int32, (TS, POOL_GROUP_DIM), 0)
             + (tile_in_row * TS + 1)).astype(F32)
    for g, w in enumerate(POOL_WINDOWS):
        sl = slice(g * POOL_GROUP_DIM, (g + 1) * POOL_GROUP_DIM)
        x0 = pool_ref[POOL_HALO:POOL_HALO + TS, sl]
        acc = x0
        for j in range(1, w):
            acc = acc + pool_ref[POOL_HALO - j:POOL_HALO - j + TS, sl]
        pooled = acc / jnp.minimum(count, float(w)) - x0
        mixed = jnp.dot(pooled.astype(BF16), wpool_ref[g], preferred_element_type=F32)
        mix_ref[:, sl] = (mixed * pscale_ref[:, sl]).astype(BF16)
    pool_ref[0:POOL_HALO, :] = pool_ref[TS:TS + POOL_HALO, :]
    fill()

    gate_c = proj_ref[:, GATE_COL:GATE_COL + LANES]
    g_c = -jnp.exp(alog_row_ref[...]) * _softplus(gate_c + dtb_row_ref[...])
    beta_c = _sigmoid(gate_c)
    r_i = lax.broadcasted_iota(jnp.int32, (TS, TS), 0)
    c_i = lax.broadcasted_iota(jnp.int32, (TS, TS), 1)
    blk_lower = ((r_i // CHUNK == c_i // CHUNK) & (c_i <= r_i)).astype(F32)
    gcum_c = jnp.dot(blk_lower, g_c, preferred_element_type=F32,
                     precision=lax.Precision.HIGHEST)

    gate_r = gt_ref[...].reshape(NC * 2 * GDN_HEADS, CHUNK)
    alog_col = jnp.concatenate([alog_col_ref[...]] * NC, axis=0)
    dtb_col = jnp.concatenate([dtb_col_ref[...]] * NC, axis=0)
    g_r = -jnp.exp(alog_col) * _softplus(gate_r + dtb_col)
    u_i = lax.broadcasted_iota(jnp.int32, (CHUNK, CHUNK), 0)
    v_i = lax.broadcasted_iota(jnp.int32, (CHUNK, CHUNK), 1)
    upper = (u_i <= v_i).astype(F32)
    gcum_r = jnp.dot(g_r, upper, preferred_element_type=F32,
                     precision=lax.Precision.HIGHEST).reshape(NC, 2 * GDN_HEADS, CHUNK)

    H = GDN_HEADS
    dh = GDN_HEAD_DIM

    def heads(x2d):
        return jnp.stack([x2d[:, hd * dh:(hd + 1) * dh] for hd in range(H)],
                         axis=0).reshape(H * NC, CHUNK, dh)

    def head_cols(x2d, off):
        return jnp.stack([x2d[:, off + hd:off + hd + 1] for hd in range(H)],
                         axis=0).reshape(H * NC, CHUNK, 1)

    def head_rows(x3d, off):
        return jnp.stack([x3d[:, off + hd:off + hd + 1, :] for hd in range(H)],
                         axis=0).reshape(H * NC, 1, CHUNK)

    fill()
    q = heads(qkv_blocks[0])
    k = heads(qkv_blocks[1])
    v = heads(qkv_blocks[2])
    qh = q * (lax.rsqrt(jnp.sum(q * q, axis=-1, keepdims=True) + EPS) * (dh ** -0.5))
    kh = k * lax.rsqrt(jnp.sum(k * k, axis=-1, keepdims=True) + EPS)

    gc = head_cols(gcum_c, 0)
    bc = head_cols(beta_c, H)
    gr = head_rows(gcum_r, 0)
    g_last = gc[:, CHUNK - 1:CHUNK, :]
    eg = jnp.exp(gc)

    ri = lax.broadcasted_iota(jnp.int32, (H * NC, CHUNK, CHUNK), 1)
    ci = lax.broadcasted_iota(jnp.int32, (H * NC, CHUNK, CHUNK), 2)
    decay_incl = jnp.exp(jnp.where(ri >= ci, gc - gr, -jnp.inf))
    decay_strict = jnp.where(ri > ci, decay_incl, 0.0)

    kh_b = kh.astype(BF16)
    m = _bdot_nt(kh_b, kh_b) * bc * decay_strict
    fill()
    t_inv = _unit_lower_inverse(m, fill)
    rhs = jnp.concatenate([v * bc, kh * (bc * eg)], axis=-1)
    sol = _bdot(t_inv, rhs)
    fill()
    per_chunk = lambda x: x.reshape((H, NC) + x.shape[1:])
    value = per_chunk(sol[:, :, :dh])
    k_cumdecay = per_chunk(sol[:, :, dh:])
    attn = per_chunk(_bdot_nt(qh, kh_b) * decay_incl)
    q_dec = per_chunk(qh * eg)
    k_dec = per_chunk(kh * jnp.exp(g_last - gc))
    state_decay = per_chunk(jnp.exp(g_last))
    fill()

    state = state_ref[...]
    outs = []
    for c in range(NC):
        lhs = jnp.concatenate([k_cumdecay[:, c], q_dec[:, c]], axis=1)
        r = _bdot(lhs, state)
        fill()
        v_new = (value[:, c] - r[:, :CHUNK]).astype(BF16)
        outs.append(r[:, CHUNK:] + _bdot(attn[:, c], v_new))
        state = state * state_decay[:, c] + jnp.einsum(
            'hcd,hce->hde', k_dec[:, c].astype(BF16), v_new, preferred_element_type=F32)
        fill()
    state_ref[...] = state

    o = jnp.concatenate(outs, axis=1)
    o = o * lax.rsqrt(jnp.mean(o * o, axis=-1, keepdims=True) + EPS) * gnorm_ref[...]
    for hd in range(H):
        z = proj_ref[:, Z_OFF + hd * dh:Z_OFF + (hd + 1) * dh]
        mix_ref[:, POOL_WIDTH + hd * dh:POOL_WIDTH + (hd + 1) * dh] = (
            o[hd] * (z * _sigmoid(z))).astype(BF16)

    hmid_dst[...] = h_ref[...] + jnp.dot(mix_ref[...], wout_ref[...],
                                         preferred_element_type=F32)


def _mlp_pieces(x_ref, nw_ref, wup_ref, wdown_ref, nf_ref, out_ref, u_ref, hid_ref, acc_ref,
                final_norm):
    n = D_FF // FF_CHUNK

    def prepare():
        x = x_ref[...]
        acc_ref[...] = x
        u_ref[...] = _rms_norm(x, nw_ref[...]).astype(BF16)

    def up(j):
        def piece():
            cols = slice(j * FF_CHUNK, (j + 1) * FF_CHUNK)
            hid = jnp.dot(u_ref[...], wup_ref[:, cols], preferred_element_type=F32)
            hid_ref[j % 2] = jnp.square(jnp.maximum(hid, 0.0)).astype(BF16)
        return piece

    def down(j):
        def piece():
            rows = slice(j * FF_CHUNK, (j + 1) * FF_CHUNK)
            acc_ref[...] += jnp.dot(hid_ref[j % 2], wdown_ref[rows, :],
                                    preferred_element_type=F32)
        return piece

    def finish():
        acc = acc_ref[...]
        out_ref[0] = _rms_norm(acc, nf_ref[...]) if final_norm else acc

    order = [prepare, up(0)]
    for j in range(1, n):
        order += [up(j), down(j - 1)]
    return order + [down(n - 1), finish]


def _layer_kernel(h_next_ref, h_first_ref, nwmix_ref, win_ref, wgt_ref, convw_ref, wpool_ref,
                  pscale_ref, alog_row_ref, dtb_row_ref, alog_col_ref, dtb_col_ref, gnorm_ref,
                  wout_ref, nwmlp_ref, wup_ref, wdown_ref, nf_ref, out_ref,
                  proj_scr, gt_scr, h_scr, hmid_scr, state_ref, conv_ref, pool_ref, mix_ref,
                  uin_scr, umlp_scr, hid_scr, acc_scr,
                  *, tiles_per_row, final_norm):
    s = pl.program_id(0)
    cur = s % 2
    nxt = 1 - cur

    @pl.when(s == 0)
    def _():
        for piece in _inproj_pieces(h_first_ref.at[0], nwmix_ref, win_ref, wgt_ref,
                                    proj_scr.at[0], gt_scr.at[0], h_scr.at[0], uin_scr):
            piece()
        hmid_scr[1] = jnp.zeros((TS, D_MODEL), F32)

    @pl.when(s % tiles_per_row == 0)
    def _():
        state_ref[...] = jnp.zeros_like(state_ref)
        conv_ref[0:CONV_HALO, :] = jnp.zeros((CONV_HALO, 3 * GDN_WIDTH), F32)
        pool_ref[0:POOL_HALO, :] = jnp.zeros((POOL_HALO, POOL_WIDTH), F32)

    mlp = _mlp_pieces(hmid_scr.at[nxt], nwmlp_ref, wup_ref, wdown_ref, nf_ref, out_ref,
                      umlp_scr, hid_scr, acc_scr, final_norm)
    inp = _inproj_pieces(h_next_ref.at[0], nwmix_ref, win_ref, wgt_ref,
                         proj_scr.at[nxt], gt_scr.at[nxt], h_scr.at[nxt], uin_scr)
    dense = iter(mlp[:1] + inp[:1] + mlp[1:-1] + inp[1:-1] + mlp[-1:] + inp[-1:])

    def fill():
        piece = next(dense, None)
        if piece is not None:
            piece()

    fill()
    fill()
    _mixer_stage(proj_scr.at[cur], gt_scr.at[cur], h_scr.at[cur], s % tiles_per_row,
                 convw_ref, wpool_ref, pscale_ref, alog_row_ref, dtb_row_ref, alog_col_ref,
                 dtb_col_ref, gnorm_ref, wout_ref, hmid_scr.at[cur],
                 state_ref, conv_ref, pool_ref, mix_ref, fill)
    for piece in dense:
        piece()


def _layer(h, nw_mix, w_main, w_gate_t, conv_w, w_pool, pool_scale, alog_row, dtb_row,
           alog_col, dtb_col, gdn_norm, w_out, nw_mlp, w_up, w_down, nf, final_norm):
    B, S, D = h.shape
    tiles_per_row = S // TS
    n_tiles = B * tiles_per_row
    const = dict(pipeline_mode=pl.Buffered(1))
    zero2 = lambda s: (0, 0)

    def tile_map(offset):
        def index_map(s):
            t = jnp.clip(s + offset, 0, n_tiles - 1)
            return (t // tiles_per_row, t % tiles_per_row, 0)
        return index_map

    return pl.pallas_call(
        functools.partial(_layer_kernel, tiles_per_row=tiles_per_row, final_norm=final_norm),
        grid=(n_tiles + 1,),
        in_specs=[
            pl.BlockSpec((1, TS, D), tile_map(1)),
            pl.BlockSpec((1, TS, D), lambda s: (0, 0, 0), **const),
            pl.BlockSpec((1, D), zero2, **const),
            pl.BlockSpec((D, PROJ_WIDTH), zero2, **const),
            pl.BlockSpec((2 * GDN_HEADS, D), zero2, **const),
            pl.BlockSpec((CONV_WIDTH, 3 * GDN_WIDTH), zero2, **const),
            pl.BlockSpec((POOL_GROUPS, POOL_GROUP_DIM, POOL_GROUP_DIM),
                         lambda s: (0, 0, 0), **const),
            pl.BlockSpec((1, POOL_WIDTH), zero2, **const),
            pl.BlockSpec((1, LANES), zero2, **const),
            pl.BlockSpec((1, LANES), zero2, **const),
            pl.BlockSpec((2 * GDN_HEADS, 1), zero2, **const),
            pl.BlockSpec((2 * GDN_HEADS, 1), zero2, **const),
            pl.BlockSpec((1, GDN_HEAD_DIM), zero2, **const),
            pl.BlockSpec((D, D), zero2, **const),
            pl.BlockSpec((1, D), zero2, **const),
            pl.BlockSpec((D, D_FF), zero2, **const),
            pl.BlockSpec((D_FF, D), zero2, **const),
            pl.BlockSpec((1, D), zero2, **const),
        ],
        out_specs=pl.BlockSpec((1, TS, D), tile_map(-1)),
        out_shape=jax.ShapeDtypeStruct((B, S, D), F32),
        scratch_shapes=[
            pltpu.VMEM((2, TS, PROJ_WIDTH), F32),
            pltpu.VMEM((2, NC, 2 * GDN_HEADS, CHUNK), F32),
            pltpu.VMEM((2, TS, D), F32),
            pltpu.VMEM((2, TS, D), F32),
            pltpu.VMEM((GDN_HEADS, GDN_HEAD_DIM, GDN_HEAD_DIM), F32),
            pltpu.VMEM((TS + CONV_HALO, 3 * GDN_WIDTH), F32),
            pltpu.VMEM((TS + POOL_HALO, POOL_WIDTH), F32),
            pltpu.VMEM((TS, D), BF16),
            pltpu.VMEM((TS, D), BF16),
            pltpu.VMEM((TS, D), BF16),
            pltpu.VMEM((2, TS, FF_CHUNK), BF16),
            pltpu.VMEM((TS, D), F32),
        ],
        compiler_params=pltpu.CompilerParams(
            dimension_semantics=("arbitrary",),
            vmem_limit_bytes=VMEM_LIMIT),
        name="layer_final" if final_norm else "layer",
    )(h, h, nw_mix, w_main, w_gate_t, conv_w, w_pool, pool_scale, alog_row, dtb_row,
      alog_col, dtb_col, gdn_norm, w_out, nw_mlp, w_up, w_down, nf)


def kernel(x, norm_mix, w_in, conv_w, w_pool, pool_scale, a_log, dt_bias, gdn_norm, w_out,
           norm_mlp, w_up, w_down, norm_final):
    depth = w_in.shape[0]
    H = GDN_HEADS
    h = x
    for l in range(depth):
        w_main = jnp.pad(w_in[l][:, :MAIN_WIDTH + 2 * H],
                         ((0, 0), (0, LANES - 2 * H))).astype(BF16)
        w_gate_t = w_in[l][:, MAIN_WIDTH:MAIN_WIDTH + 2 * H].T.astype(BF16)
        pad_row = lambda p: jnp.pad(p, (0, LANES - H)).reshape(1, LANES)
        pad_col = lambda p: jnp.pad(p, (0, H)).reshape(2 * H, 1)
        h = _layer(h, norm_mix[l].reshape(1, D_MODEL), w_main, w_gate_t, conv_w[l],
                   w_pool[l].astype(BF16), pool_scale[l].reshape(1, POOL_WIDTH),
                   pad_row(a_log[l]), pad_row(dt_bias[l]), pad_col(a_log[l]), pad_col(dt_bias[l]),
                   gdn_norm[l].reshape(1, GDN_HEAD_DIM), w_out[l].astype(BF16),
                   norm_mlp[l].reshape(1, D_MODEL), w_up[l].astype(BF16), w_down[l].astype(BF16),
                   norm_final.reshape(1, D_MODEL), final_norm=(l == depth - 1))
    return h
```

```python
import functools

import jax
import jax.numpy as jnp
from jax import lax
from jax.experimental import pallas as pl
from jax.experimental.pallas import tpu as pltpu

D_MODEL = 1024
CHUNK = 64
POOL_WIDTH = D_MODEL // 2
POOL_GROUPS = 4
POOL_GROUP_DIM = POOL_WIDTH // POOL_GROUPS
POOL_WINDOWS = (2, 4, 8, 16)
GDN_WIDTH = D_MODEL - POOL_WIDTH
GDN_HEADS = 4
GDN_HEAD_DIM = GDN_WIDTH // GDN_HEADS
CONV_WIDTH = 4
D_FF = 4 * D_MODEL
EPS = 1e-6

SUBLANES = 8
MAIN_WIDTH = POOL_WIDTH + 4 * GDN_WIDTH
PROJ_WIDTH = MAIN_WIDTH
QKV_OFF = POOL_WIDTH
Z_OFF = POOL_WIDTH + 3 * GDN_WIDTH
CONV_HALO = SUBLANES
POOL_HALO = 16
INV_BASE = 16

TS = 256
NC = TS // CHUNK
FF_CHUNK = 512
PROJ_CHUNK = 256
VMEM_LIMIT = 58 * 1024 * 1024

F32 = jnp.float32
BF16 = jnp.bfloat16


def _rms_norm(x, w):
    ms = jnp.mean(x * x, axis=-1, keepdims=True)
    return x * lax.rsqrt(ms + EPS) * w


def _softplus(x):
    return jnp.maximum(x, 0.0) + jnp.log1p(jnp.exp(-jnp.abs(x)))


def _sigmoid(x):
    return 1.0 / (1.0 + jnp.exp(-x))


def _bdot(a, b):
    return jnp.einsum('cik,ckj->cij', a.astype(BF16), b.astype(BF16),
                      preferred_element_type=F32)


def _bdot_nt(a, b):
    return jnp.einsum('cid,cjd->cij', a.astype(BF16), b.astype(BF16),
                      preferred_element_type=F32)


def _unit_lower_inverse(m, fill):
    ri = lax.broadcasted_iota(jnp.int32, m.shape, 1)
    ci = lax.broadcasted_iota(jnp.int32, m.shape, 2)
    eye = (ri == ci).astype(F32)
    d = jnp.where(ri // INV_BASE == ci // INV_BASE, m, 0.0)
    t = eye - d
    p = _bdot(d, d)
    fill()
    power = 2
    while power < INV_BASE:
        t = t + _bdot(t, p)
        power *= 2
        if power < INV_BASE:
            p = _bdot(p, p)
        fill()
    b = INV_BASE
    while b < CHUNK:
        lower_left = ((ri // (2 * b) == ci // (2 * b))
                      & ((ri // b) % 2 == 1) & ((ci // b) % 2 == 0))
        c = jnp.where(lower_left, m, 0.0)
        ct = _bdot(c, t)
        fill()
        t = t - _bdot(t, ct)
        b *= 2
        fill()
    return t


def _inproj_pieces(x_ref, nw_ref, w_ref, wgt_ref, proj_dst, gt_dst, h_dst, u_ref):
    def prepare():
        x = x_ref[...]
        h_dst[...] = x
        u_ref[...] = _rms_norm(x, nw_ref[...]).astype(BF16)

    def project(cols):
        def piece():
            proj_dst[:, cols] = jnp.dot(u_ref[...], w_ref[:, cols], preferred_element_type=F32)
        return piece

    def gates():
        gt = lax.dot_general(wgt_ref[...], u_ref[...], (((1,), (1,)), ((), ())),
                             preferred_element_type=F32)
        for c in range(NC):
            gt_dst[c] = gt[:, c * CHUNK:(c + 1) * CHUNK]

    starts = range(0, PROJ_WIDTH, PROJ_CHUNK)
    return ([prepare] + [project(slice(c0, min(c0 + PROJ_CHUNK, PROJ_WIDTH))) for c0 in starts]
            + [gates])


def _mixer_stage(proj_ref, gt_ref, h_ref, tile_in_row, convw_ref, wpool_ref, pscale_ref,
                 alog_col_ref, dtb_col_ref, gnorm_ref, wout_ref,
                 hmid_dst, state_ref, conv_ref, pool_ref, mix_ref, fill):
    conv_ref[CONV_HALO:CONV_HALO + TS, :] = proj_ref[:, QKV_OFF:QKV_OFF + 3 * GDN_WIDTH]
    qkv_blocks = []
    half = TS // 2
    for blk in range(3):
        cs = slice(blk * GDN_WIDTH, (blk + 1) * GDN_WIDTH)
        parts = []
        for r0 in (CONV_HALO, CONV_HALO + half):
            y = conv_ref[r0:r0 + half, cs] * convw_ref[CONV_WIDTH - 1:CONV_WIDTH, cs]
            for j in range(1, CONV_WIDTH):
                y = y + (conv_ref[r0 - j:r0 - j + half, cs]
                         * convw_ref[CONV_WIDTH - 1 - j:CONV_WIDTH - j, cs])
            parts.append(y * _sigmoid(y))
            fill()
        qkv_blocks.append(jnp.concatenate(parts, axis=0))
    conv_ref[0:CONV_HALO, :] = conv_ref[TS:TS + CONV_HALO, :]

    pool_ref[POOL_HALO:POOL_HALO + TS, :] = proj_ref[:, 0:POOL_WIDTH]
    count = (lax.broadcasted_iota(jnp.int32, (TS, POOL_GROUP_DIM), 0)
             + (tile_in_row * TS + 1)).astype(F32)
    for g, w in enumerate(POOL_WINDOWS):
        sl = slice(g * POOL_GROUP_DIM, (g + 1) * POOL_GROUP_DIM)
        x0 = pool_ref[POOL_HALO:POOL_HALO + TS, sl]
        acc = x0
        for j in range(1, w):
            acc = acc + pool_ref[POOL_HALO - j:POOL_HALO - j + TS, sl]
        pooled = acc / jnp.minimum(count, float(w)) - x0
        mixed = jnp.dot(pooled.astype(BF16), wpool_ref[g], preferred_element_type=F32)
        mix_ref[:, sl] = (mixed * pscale_ref[:, sl]).astype(BF16)
    pool_ref[0:POOL_HALO, :] = pool_ref[TS:TS + POOL_HALO, :]
    fill()

    gate_r = gt_ref[...].reshape(NC * 2 * GDN_HEADS, CHUNK)
    alog_col = jnp.concatenate([alog_col_ref[...]] * NC, axis=0)
    dtb_col = jnp.concatenate([dtb_col_ref[...]] * NC, axis=0)
    g_r = -jnp.exp(alog_col) * _softplus(gate_r + dtb_col)
    beta_r = _sigmoid(gate_r)
    u_i = lax.broadcasted_iota(jnp.int32, (CHUNK, CHUNK), 0)
    v_i = lax.broadcasted_iota(jnp.int32, (CHUNK, CHUNK), 1)
    upper = (u_i <= v_i).astype(F32)
    gcum_r = jnp.dot(g_r, upper, preferred_element_type=F32,
                     precision=lax.Precision.HIGHEST)
    gates_t = jnp.concatenate([gcum_r, beta_r], axis=1).T
    gcum_r = gcum_r.reshape(NC, 2 * GDN_HEADS, CHUNK)

    H = GDN_HEADS
    dh = GDN_HEAD_DIM

    def heads(x2d):
        return jnp.stack([x2d[:, hd * dh:(hd + 1) * dh] for hd in range(H)],
                         axis=0).reshape(H * NC, CHUNK, dh)

    def head_cols(row0, off):
        return jnp.stack([gates_t[row0:row0 + CHUNK, c * 2 * H + off + hd:c * 2 * H + off + hd + 1]
                          for hd in range(H) for c in range(NC)], axis=0)

    def head_rows(x3d, off):
        return jnp.stack([x3d[:, off + hd:off + hd + 1, :] for hd in range(H)],
                         axis=0).reshape(H * NC, 1, CHUNK)

    fill()
    q = heads(qkv_blocks[0])
    k = heads(qkv_blocks[1])
    v = heads(qkv_blocks[2])
    qh = q * (lax.rsqrt(jnp.sum(q * q, axis=-1, keepdims=True) + EPS) * (dh ** -0.5))
    kh = k * lax.rsqrt(jnp.sum(k * k, axis=-1, keepdims=True) + EPS)

    gc = head_cols(0, 0)
    bc = head_cols(CHUNK, H)
    gr = head_rows(gcum_r, 0)
    g_last = gc[:, CHUNK - 1:CHUNK, :]
    eg = jnp.exp(gc)

    ri = lax.broadcasted_iota(jnp.int32, (H * NC, CHUNK, CHUNK), 1)
    ci = lax.broadcasted_iota(jnp.int32, (H * NC, CHUNK, CHUNK), 2)
    decay_incl = jnp.exp(jnp.where(ri >= ci, gc - gr, -jnp.inf))
    decay_strict = jnp.where(ri > ci, decay_incl, 0.0)

    kh_b = kh.astype(BF16)
    m = _bdot_nt(kh_b, kh_b) * bc * decay_strict
    fill()
    t_inv = _unit_lower_inverse(m, fill)
    rhs = jnp.concatenate([v * bc, kh * (bc * eg)], axis=-1)
    sol = _bdot(t_inv, rhs)
    fill()
    per_chunk = lambda x: x.reshape((H, NC) + x.shape[1:])
    value = per_chunk(sol[:, :, :dh])
    k_cumdecay = per_chunk(sol[:, :, dh:])
    attn = per_chunk(_bdot_nt(qh, kh_b) * decay_incl)
    q_dec = per_chunk(qh * eg)
    k_dec = per_chunk(kh * jnp.exp(g_last - gc))
    state_decay = per_chunk(jnp.exp(g_last))
    fill()

    state = state_ref[...]
    outs = []
    for c in range(NC):
        lhs = jnp.concatenate([k_cumdecay[:, c], q_dec[:, c]], axis=1)
        r = _bdot(lhs, state)
        fill()
        v_new = (value[:, c] - r[:, :CHUNK]).astype(BF16)
        outs.append(r[:, CHUNK:] + _bdot(attn[:, c], v_new))
        state = state * state_decay[:, c] + jnp.einsum(
            'hcd,hce->hde', k_dec[:, c].astype(BF16), v_new, preferred_element_type=F32)
        fill()
    state_ref[...] = state

    o = jnp.concatenate(outs, axis=1)
    o = o * lax.rsqrt(jnp.mean(o * o, axis=-1, keepdims=True) + EPS) * gnorm_ref[...]
    for hd in range(H):
        z = proj_ref[:, Z_OFF + hd * dh:Z_OFF + (hd + 1) * dh]
        mix_ref[:, POOL_WIDTH + hd * dh:POOL_WIDTH + (hd + 1) * dh] = (
            o[hd] * (z * _sigmoid(z))).astype(BF16)

    hmid_dst[...] = h_ref[...] + jnp.dot(mix_ref[...], wout_ref[...],
                                         preferred_element_type=F32)


def _mlp_pieces(x_ref, nw_ref, wup_ref, wdown_ref, nf_ref, out_ref, u_ref, hid_ref, acc_ref,
                final_norm):
    n = D_FF // FF_CHUNK

    def prepare():
        x = x_ref[...]
        acc_ref[...] = x
        u_ref[...] = _rms_norm(x, nw_ref[...]).astype(BF16)

    def up(j):
        def piece():
            cols = slice(j * FF_CHUNK, (j + 1) * FF_CHUNK)
            hid = jnp.dot(u_ref[...], wup_ref[:, cols], preferred_element_type=F32)
            hid_ref[j % 2] = jnp.square(jnp.maximum(hid, 0.0)).astype(BF16)
        return piece

    def down(j):
        def piece():
            rows = slice(j * FF_CHUNK, (j + 1) * FF_CHUNK)
            acc_ref[...] += jnp.dot(hid_ref[j % 2], wdown_ref[rows, :],
                                    preferred_element_type=F32)
        return piece

    def finish():
        acc = acc_ref[...]
        out_ref[0] = _rms_norm(acc, nf_ref[...]) if final_norm else acc

    order = [prepare, up(0)]
    for j in range(1, n):
        order += [up(j), down(j - 1)]
    return order + [down(n - 1), finish]


def _layer_kernel(h_next_ref, h_first_ref, nwmix_ref, win_ref, wgt_ref, convw_ref, wpool_ref,
                  pscale_ref, alog_col_ref, dtb_col_ref, gnorm_ref,
                  wout_ref, nwmlp_ref, wup_ref, wdown_ref, nf_ref, out_ref,
                  proj_scr, gt_scr, h_scr, hmid_scr, state_ref, conv_ref, pool_ref, mix_ref,
                  uin_scr, umlp_scr, hid_scr, acc_scr,
                  *, tiles_per_row, final_norm):
    s = pl.program_id(0)
    cur = s % 2
    nxt = 1 - cur

    @pl.when(s == 0)
    def _():
        for piece in _inproj_pieces(h_first_ref.at[0], nwmix_ref, win_ref, wgt_ref,
                                    proj_scr.at[0], gt_scr.at[0], h_scr.at[0], uin_scr):
            piece()
        hmid_scr[1] = jnp.zeros((TS, D_MODEL), F32)

    @pl.when(s % tiles_per_row == 0)
    def _():
        state_ref[...] = jnp.zeros_like(state_ref)
        conv_ref[0:CONV_HALO, :] = jnp.zeros((CONV_HALO, 3 * GDN_WIDTH), F32)
        pool_ref[0:POOL_HALO, :] = jnp.zeros((POOL_HALO, POOL_WIDTH), F32)

    mlp = _mlp_pieces(hmid_scr.at[nxt], nwmlp_ref, wup_ref, wdown_ref, nf_ref, out_ref,
                      umlp_scr, hid_scr, acc_scr, final_norm)
    inp = _inproj_pieces(h_next_ref.at[0], nwmix_ref, win_ref, wgt_ref,
                         proj_scr.at[nxt], gt_scr.at[nxt], h_scr.at[nxt], uin_scr)
    dense = iter(mlp[:1] + inp[:1] + mlp[1:-1] + inp[1:-1] + mlp[-1:] + inp[-1:])

    def fill():
        piece = next(dense, None)
        if piece is not None:
            piece()

    fill()
    fill()
    _mixer_stage(proj_scr.at[cur], gt_scr.at[cur], h_scr.at[cur], s % tiles_per_row,
                 convw_ref, wpool_ref, pscale_ref, alog_col_ref,
                 dtb_col_ref, gnorm_ref, wout_ref, hmid_scr.at[cur],
                 state_ref, conv_ref, pool_ref, mix_ref, fill)
    for piece in dense:
        piece()


def _layer(layer, h, params, nf, final_norm):
    B, S, D = h.shape
    tiles_per_row = S // TS
    n_tiles = B * tiles_per_row

    def tile_map(offset):
        def index_map(s):
            t = jnp.clip(s + offset, 0, n_tiles - 1)
            return (t // tiles_per_row, t % tiles_per_row, 0)
        return index_map

    def layer_spec(p, block=None):
        block = tuple(p.shape[1:]) if block is None else block
        zeros = (0,) * len(block)
        return pl.BlockSpec((None,) + block, lambda s: (layer,) + zeros,
                            pipeline_mode=pl.Buffered(1))

    (nw_mix, w_in, w_gate_t, conv_w, w_pool, pool_scale, alog_col, dtb_col, gdn_norm, w_out,
     nw_mlp, w_up, w_down) = params
    param_specs = [layer_spec(p) for p in params]
    param_specs[1] = layer_spec(w_in, (D, PROJ_WIDTH))

    return pl.pallas_call(
        functools.partial(_layer_kernel, tiles_per_row=tiles_per_row, final_norm=final_norm),
        grid=(n_tiles + 1,),
        in_specs=[
            pl.BlockSpec((1, TS, D), tile_map(1)),
            pl.BlockSpec((1, TS, D), lambda s: (0, 0, 0), pipeline_mode=pl.Buffered(1)),
            *param_specs,
            pl.BlockSpec((1, D), lambda s: (0, 0), pipeline_mode=pl.Buffered(1)),
        ],
        out_specs=pl.BlockSpec((1, TS, D), tile_map(-1)),
        out_shape=jax.ShapeDtypeStruct((B, S, D), F32),
        scratch_shapes=[
            pltpu.VMEM((2, TS, PROJ_WIDTH), F32),
            pltpu.VMEM((2, NC, 2 * GDN_HEADS, CHUNK), F32),
            pltpu.VMEM((2, TS, D), F32),
            pltpu.VMEM((2, TS, D), F32),
            pltpu.VMEM((GDN_HEADS, GDN_HEAD_DIM, GDN_HEAD_DIM), F32),
            pltpu.VMEM((TS + CONV_HALO, 3 * GDN_WIDTH), F32),
            pltpu.VMEM((TS + POOL_HALO, POOL_WIDTH), F32),
            pltpu.VMEM((TS, D), BF16),
            pltpu.VMEM((TS, D), BF16),
            pltpu.VMEM((TS, D), BF16),
            pltpu.VMEM((2, TS, FF_CHUNK), BF16),
            pltpu.VMEM((TS, D), F32),
        ],
        compiler_params=pltpu.CompilerParams(
            dimension_semantics=("arbitrary",),
            vmem_limit_bytes=VMEM_LIMIT),
        name="layer_final" if final_norm else "layer",
    )(h, h, *params, nf)


def kernel(x, norm_mix, w_in, conv_w, w_pool, pool_scale, a_log, dt_bias, gdn_norm, w_out,
           norm_mlp, w_up, w_down, norm_final):
    depth = w_in.shape[0]
    H = GDN_HEADS
    row = lambda p: p.reshape(depth, 1, p.shape[-1])
    col = lambda p: jnp.pad(p, ((0, 0), (0, H))).reshape(depth, 2 * H, 1)
    params = (
        row(norm_mix),
        w_in.astype(BF16),
        jnp.swapaxes(w_in[:, :, MAIN_WIDTH:], 1, 2).astype(BF16),
        conv_w,
        w_pool.astype(BF16),
        row(pool_scale),
        col(a_log),
        col(dt_bias),
        row(gdn_norm),
        w_out.astype(BF16),
        row(norm_mlp),
        w_up.astype(BF16),
        w_down.astype(BF16),
    )
    nf = norm_final.reshape(1, D_MODEL)
    h = x
    for l in range(depth):
        h = _layer(l, h, params, nf, final_norm=(l == depth - 1))
    return h
```

```python
import functools

import jax
import jax.numpy as jnp
from jax import lax
from jax.experimental import pallas as pl
from jax.experimental.pallas import tpu as pltpu

D_MODEL = 1024
CHUNK = 64
POOL_WIDTH = D_MODEL // 2
POOL_GROUPS = 4
POOL_GROUP_DIM = POOL_WIDTH // POOL_GROUPS
POOL_WINDOWS = (2, 4, 8, 16)
GDN_WIDTH = D_MODEL - POOL_WIDTH
GDN_HEADS = 4
GDN_HEAD_DIM = GDN_WIDTH // GDN_HEADS
CONV_WIDTH = 4
D_FF = 4 * D_MODEL
EPS = 1e-6

SUBLANES = 8
MAIN_WIDTH = POOL_WIDTH + 4 * GDN_WIDTH
PROJ_WIDTH = MAIN_WIDTH
QKV_OFF = POOL_WIDTH
Z_OFF = POOL_WIDTH + 3 * GDN_WIDTH
CONV_HALO = SUBLANES
POOL_HALO = 16
INV_BASE = 16

TS = 256
NC = TS // CHUNK
FF_CHUNK = 512
PROJ_CHUNK = 256
WEIGHT_CHUNK_BYTES = 2 * 1024 * 1024
VMEM_LIMIT = 58 * 1024 * 1024

F32 = jnp.float32
BF16 = jnp.bfloat16


def _rms_norm(x, w):
    ms = jnp.mean(x * x, axis=-1, keepdims=True)
    return x * lax.rsqrt(ms + EPS) * w


def _softplus(x):
    return jnp.maximum(x, 0.0) + jnp.log1p(jnp.exp(-jnp.abs(x)))


def _sigmoid(x):
    return 1.0 / (1.0 + jnp.exp(-x))


def _bdot(a, b):
    return jnp.einsum('cik,ckj->cij', a.astype(BF16), b.astype(BF16),
                      preferred_element_type=F32)


def _bdot_nt(a, b):
    return jnp.einsum('cid,cjd->cij', a.astype(BF16), b.astype(BF16),
                      preferred_element_type=F32)


def _unit_lower_inverse(m, fill):
    ri = lax.broadcasted_iota(jnp.int32, m.shape, 1)
    ci = lax.broadcasted_iota(jnp.int32, m.shape, 2)
    eye = (ri == ci).astype(F32)
    d = jnp.where(ri // INV_BASE == ci // INV_BASE, m, 0.0)
    t = eye - d
    p = _bdot(d, d)
    fill()
    power = 2
    while power < INV_BASE:
        t = t + _bdot(t, p)
        power *= 2
        if power < INV_BASE:
            p = _bdot(p, p)
        fill()
    b = INV_BASE
    while b < CHUNK:
        lower_left = ((ri // (2 * b) == ci // (2 * b))
                      & ((ri // b) % 2 == 1) & ((ci // b) % 2 == 0))
        c = jnp.where(lower_left, m, 0.0)
        ct = _bdot(c, t)
        fill()
        t = t - _bdot(t, ct)
        b *= 2
        fill()
    return t


def _inproj_pieces(x_ref, nw_ref, w_ref, wgt_ref, proj_dst, gt_dst, h_dst, u_ref):
    def prepare():
        x = x_ref[...]
        h_dst[...] = x
        u_ref[...] = _rms_norm(x, nw_ref[...]).astype(BF16)

    def project(cols):
        def piece():
            proj_dst[:, cols] = jnp.dot(u_ref[...], w_ref[:, cols], preferred_element_type=F32)
        return piece

    def gates():
        gt = lax.dot_general(wgt_ref[...], u_ref[...], (((1,), (1,)), ((), ())),
                             preferred_element_type=F32)
        for c in range(NC):
            gt_dst[c] = gt[:, c * CHUNK:(c + 1) * CHUNK]

    starts = range(0, PROJ_WIDTH, PROJ_CHUNK)
    return ([prepare] + [project(slice(c0, min(c0 + PROJ_CHUNK, PROJ_WIDTH))) for c0 in starts]
            + [gates])


def _mixer_stage(proj_ref, gt_ref, h_ref, tile_in_row, convw_ref, wpool_ref, pscale_ref,
                 alog_col_ref, dtb_col_ref, gnorm_ref, wout_ref,
                 hmid_dst, state_ref, conv_ref, pool_ref, mix_ref, fill):
    conv_ref[CONV_HALO:CONV_HALO + TS, :] = proj_ref[:, QKV_OFF:QKV_OFF + 3 * GDN_WIDTH]
    qkv_blocks = []
    half = TS // 2
    for blk in range(3):
        cs = slice(blk * GDN_WIDTH, (blk + 1) * GDN_WIDTH)
        parts = []
        for r0 in (CONV_HALO, CONV_HALO + half):
            y = conv_ref[r0:r0 + half, cs] * convw_ref[CONV_WIDTH - 1:CONV_WIDTH, cs]
            for j in range(1, CONV_WIDTH):
                y = y + (conv_ref[r0 - j:r0 - j + half, cs]
                         * convw_ref[CONV_WIDTH - 1 - j:CONV_WIDTH - j, cs])
            parts.append(y * _sigmoid(y))
            fill()
        qkv_blocks.append(jnp.concatenate(parts, axis=0))
    conv_ref[0:CONV_HALO, :] = conv_ref[TS:TS + CONV_HALO, :]

    pool_ref[POOL_HALO:POOL_HALO + TS, :] = proj_ref[:, 0:POOL_WIDTH]
    count = (lax.broadcasted_iota(jnp.int32, (TS, POOL_GROUP_DIM), 0)
             + (tile_in_row * TS + 1)).astype(F32)
    for g, w in enumerate(POOL_WINDOWS):
        sl = slice(g * POOL_GROUP_DIM, (g + 1) * POOL_GROUP_DIM)
        x0 = pool_ref[POOL_HALO:POOL_HALO + TS, sl]
        acc = x0
        for j in range(1, w):
            acc = acc + pool_ref[POOL_HALO - j:POOL_HALO - j + TS, sl]
        pooled = acc / jnp.minimum(count, float(w)) - x0
        mixed = jnp.dot(pooled.astype(BF16), wpool_ref[g], preferred_element_type=F32)
        mix_ref[:, sl] = (mixed * pscale_ref[:, sl]).astype(BF16)
    pool_ref[0:POOL_HALO, :] = pool_ref[TS:TS + POOL_HALO, :]
    fill()

    gate_r = gt_ref[...].reshape(NC * 2 * GDN_HEADS, CHUNK)
    alog_col = jnp.concatenate([alog_col_ref[...]] * NC, axis=0)
    dtb_col = jnp.concatenate([dtb_col_ref[...]] * NC, axis=0)
    g_r = -jnp.exp(alog_col) * _softplus(gate_r + dtb_col)
    beta_r = _sigmoid(gate_r)
    u_i = lax.broadcasted_iota(jnp.int32, (CHUNK, CHUNK), 0)
    v_i = lax.broadcasted_iota(jnp.int32, (CHUNK, CHUNK), 1)
    upper = (u_i <= v_i).astype(F32)
    gcum_r = jnp.dot(g_r, upper, preferred_element_type=F32,
                     precision=lax.Precision.HIGHEST)
    gates_t = jnp.concatenate([gcum_r, beta_r], axis=1).T
    gcum_r = gcum_r.reshape(NC, 2 * GDN_HEADS, CHUNK)

    H = GDN_HEADS
    dh = GDN_HEAD_DIM

    def heads(x2d):
        return jnp.stack([x2d[:, hd * dh:(hd + 1) * dh] for hd in range(H)],
                         axis=0).reshape(H * NC, CHUNK, dh)

    def head_cols(row0, off):
        return jnp.stack([gates_t[row0:row0 + CHUNK, c * 2 * H + off + hd:c * 2 * H + off + hd + 1]
                          for hd in range(H) for c in range(NC)], axis=0)

    def head_rows(x3d, off):
        return jnp.stack([x3d[:, off + hd:off + hd + 1, :] for hd in range(H)],
                         axis=0).reshape(H * NC, 1, CHUNK)

    fill()
    q = heads(qkv_blocks[0])
    k = heads(qkv_blocks[1])
    v = heads(qkv_blocks[2])
    qh = q * (lax.rsqrt(jnp.sum(q * q, axis=-1, keepdims=True) + EPS) * (dh ** -0.5))
    kh = k * lax.rsqrt(jnp.sum(k * k, axis=-1, keepdims=True) + EPS)

    gc = head_cols(0, 0)
    bc = head_cols(CHUNK, H)
    gr = head_rows(gcum_r, 0)
    g_last = gc[:, CHUNK - 1:CHUNK, :]
    eg = jnp.exp(gc)

    ri = lax.broadcasted_iota(jnp.int32, (H * NC, CHUNK, CHUNK), 1)
    ci = lax.broadcasted_iota(jnp.int32, (H * NC, CHUNK, CHUNK), 2)
    decay_incl = jnp.exp(jnp.where(ri >= ci, gc - gr, -jnp.inf))
    decay_strict = jnp.where(ri > ci, decay_incl, 0.0)

    kh_b = kh.astype(BF16)
    m = _bdot_nt(kh_b, kh_b) * bc * decay_strict
    fill()
    t_inv = _unit_lower_inverse(m, fill)
    rhs = jnp.concatenate([v * bc, kh * (bc * eg)], axis=-1)
    sol = _bdot(t_inv, rhs)
    fill()
    per_chunk = lambda x: x.reshape((H, NC) + x.shape[1:])
    value = per_chunk(sol[:, :, :dh])
    k_cumdecay = per_chunk(sol[:, :, dh:])
    attn = per_chunk(_bdot_nt(qh, kh_b) * decay_incl)
    q_dec = per_chunk(qh * eg)
    k_dec = per_chunk(kh * jnp.exp(g_last - gc))
    state_decay = per_chunk(jnp.exp(g_last))
    fill()

    state = state_ref[...]
    outs = []
    for c in range(NC):
        lhs = jnp.concatenate([k_cumdecay[:, c], q_dec[:, c]], axis=1)
        r = _bdot(lhs, state)
        fill()
        v_new = (value[:, c] - r[:, :CHUNK]).astype(BF16)
        outs.append(r[:, CHUNK:] + _bdot(attn[:, c], v_new))
        state = state * state_decay[:, c] + jnp.einsum(
            'hcd,hce->hde', k_dec[:, c].astype(BF16), v_new, preferred_element_type=F32)
        fill()
    state_ref[...] = state

    o = jnp.concatenate(outs, axis=1)
    o = o * lax.rsqrt(jnp.mean(o * o, axis=-1, keepdims=True) + EPS) * gnorm_ref[...]
    for hd in range(H):
        z = proj_ref[:, Z_OFF + hd * dh:Z_OFF + (hd + 1) * dh]
        mix_ref[:, POOL_WIDTH + hd * dh:POOL_WIDTH + (hd + 1) * dh] = (
            o[hd] * (z * _sigmoid(z))).astype(BF16)

    hmid_dst[...] = h_ref[...] + jnp.dot(mix_ref[...], wout_ref[...],
                                         preferred_element_type=F32)


def _mlp_pieces(x_ref, nw_ref, wup_ref, wdown_ref, nf_ref, out_ref, u_ref, hid_ref, acc_ref,
                final_norm):
    n = D_FF // FF_CHUNK

    def prepare():
        x = x_ref[...]
        acc_ref[...] = x
        u_ref[...] = _rms_norm(x, nw_ref[...]).astype(BF16)

    def up(j):
        def piece():
            cols = slice(j * FF_CHUNK, (j + 1) * FF_CHUNK)
            hid = jnp.dot(u_ref[...], wup_ref[:, cols], preferred_element_type=F32)
            hid_ref[j % 2] = jnp.square(jnp.maximum(hid, 0.0)).astype(BF16)
        return piece

    def down(j):
        def piece():
            rows = slice(j * FF_CHUNK, (j + 1) * FF_CHUNK)
            acc_ref[...] += jnp.dot(hid_ref[j % 2], wdown_ref[rows, :],
                                    preferred_element_type=F32)
        return piece

    def finish():
        acc = acc_ref[...]
        out_ref[0] = _rms_norm(acc, nf_ref[...]) if final_norm else acc

    order = [prepare, up(0)]
    for j in range(1, n):
        order += [up(j), down(j - 1)]
    return order + [down(n - 1), finish]


def _load_weight(w_hbm, layer, dst):
    n_rows, n_cols = dst.shape
    rows_per_chunk = 1 << ((WEIGHT_CHUNK_BYTES // (4 * n_cols)).bit_length() - 1)
    assert n_rows % rows_per_chunk == 0 and rows_per_chunk % SUBLANES == 0
    n_chunks = n_rows // rows_per_chunk

    def scoped(stage, sem):
        def copy(i):
            return pltpu.make_async_copy(
                w_hbm.at[layer, pl.ds(i * rows_per_chunk, rows_per_chunk), pl.ds(0, n_cols)],
                stage.at[i % 2], sem.at[i % 2])

        copy(0).start()
        for i in range(n_chunks):
            if i + 1 < n_chunks:
                copy(i + 1).start()
            copy(i).wait()
            dst[i * rows_per_chunk:(i + 1) * rows_per_chunk, :] = stage[i % 2].astype(BF16)

    pl.run_scoped(scoped, pltpu.VMEM((2, rows_per_chunk, n_cols), F32),
                  pltpu.SemaphoreType.DMA((2,)))


def _layer_kernel(h_next_ref, h_first_ref, nwmix_ref, win_hbm, wgt_ref, convw_ref, wpool_ref,
                  pscale_ref, alog_col_ref, dtb_col_ref, gnorm_ref,
                  wout_hbm, nwmlp_ref, wup_hbm, wdown_hbm, nf_ref, out_ref,
                  proj_scr, gt_scr, h_scr, hmid_scr, state_ref, conv_ref, pool_ref, mix_ref,
                  uin_scr, umlp_scr, hid_scr, acc_scr, win_ref, wout_ref, wup_ref, wdown_ref,
                  *, layer, tiles_per_row, final_norm):
    s = pl.program_id(0)
    cur = s % 2
    nxt = 1 - cur

    @pl.when(s == 0)
    def _():
        for w_hbm, dst in ((win_hbm, win_ref), (wout_hbm, wout_ref), (wup_hbm, wup_ref),
                           (wdown_hbm, wdown_ref)):
            _load_weight(w_hbm, layer, dst)
        for piece in _inproj_pieces(h_first_ref.at[0], nwmix_ref, win_ref, wgt_ref,
                                    proj_scr.at[0], gt_scr.at[0], h_scr.at[0], uin_scr):
            piece()
        hmid_scr[1] = jnp.zeros((TS, D_MODEL), F32)

    @pl.when(s % tiles_per_row == 0)
    def _():
        state_ref[...] = jnp.zeros_like(state_ref)
        conv_ref[0:CONV_HALO, :] = jnp.zeros((CONV_HALO, 3 * GDN_WIDTH), F32)
        pool_ref[0:POOL_HALO, :] = jnp.zeros((POOL_HALO, POOL_WIDTH), F32)

    mlp = _mlp_pieces(hmid_scr.at[nxt], nwmlp_ref, wup_ref, wdown_ref, nf_ref, out_ref,
                      umlp_scr, hid_scr, acc_scr, final_norm)
    inp = _inproj_pieces(h_next_ref.at[0], nwmix_ref, win_ref, wgt_ref,
                         proj_scr.at[nxt], gt_scr.at[nxt], h_scr.at[nxt], uin_scr)
    dense = iter(mlp[:1] + inp[:1] + mlp[1:-1] + inp[1:-1] + mlp[-1:] + inp[-1:])

    def fill():
        piece = next(dense, None)
        if piece is not None:
            piece()

    fill()
    fill()
    _mixer_stage(proj_scr.at[cur], gt_scr.at[cur], h_scr.at[cur], s % tiles_per_row,
                 convw_ref, wpool_ref, pscale_ref, alog_col_ref,
                 dtb_col_ref, gnorm_ref, wout_ref, hmid_scr.at[cur],
                 state_ref, conv_ref, pool_ref, mix_ref, fill)
    for piece in dense:
        piece()


def _layer(layer, h, params, nf, final_norm):
    B, S, D = h.shape
    tiles_per_row = S // TS
    n_tiles = B * tiles_per_row

    def tile_map(offset):
        def index_map(s):
            t = jnp.clip(s + offset, 0, n_tiles - 1)
            return (t // tiles_per_row, t % tiles_per_row, 0)
        return index_map

    def layer_spec(p):
        block = tuple(p.shape[1:])
        zeros = (0,) * len(block)
        return pl.BlockSpec((None,) + block, lambda s: (layer,) + zeros,
                            pipeline_mode=pl.Buffered(1))

    in_hbm = (1, 9, 11, 12)
    param_specs = [pl.BlockSpec(memory_space=pl.ANY) if i in in_hbm else layer_spec(p)
                   for i, p in enumerate(params)]

    return pl.pallas_call(
        functools.partial(_layer_kernel, layer=layer, tiles_per_row=tiles_per_row,
                          final_norm=final_norm),
        grid=(n_tiles + 1,),
        in_specs=[
            pl.BlockSpec((1, TS, D), tile_map(1)),
            pl.BlockSpec((1, TS, D), lambda s: (0, 0, 0), pipeline_mode=pl.Buffered(1)),
            *param_specs,
            pl.BlockSpec((1, D), lambda s: (0, 0), pipeline_mode=pl.Buffered(1)),
        ],
        out_specs=pl.BlockSpec((1, TS, D), tile_map(-1)),
        out_shape=jax.ShapeDtypeStruct((B, S, D), F32),
        scratch_shapes=[
            pltpu.VMEM((2, TS, PROJ_WIDTH), F32),
            pltpu.VMEM((2, NC, 2 * GDN_HEADS, CHUNK), F32),
            pltpu.VMEM((2, TS, D), F32),
            pltpu.VMEM((2, TS, D), F32),
            pltpu.VMEM((GDN_HEADS, GDN_HEAD_DIM, GDN_HEAD_DIM), F32),
            pltpu.VMEM((TS + CONV_HALO, 3 * GDN_WIDTH), F32),
            pltpu.VMEM((TS + POOL_HALO, POOL_WIDTH), F32),
            pltpu.VMEM((TS, D), BF16),
            pltpu.VMEM((TS, D), BF16),
            pltpu.VMEM((TS, D), BF16),
            pltpu.VMEM((2, TS, FF_CHUNK), BF16),
            pltpu.VMEM((TS, D), F32),
            pltpu.VMEM((D, PROJ_WIDTH), BF16),
            pltpu.VMEM((D, D), BF16),
            pltpu.VMEM((D, D_FF), BF16),
            pltpu.VMEM((D_FF, D), BF16),
        ],
        compiler_params=pltpu.CompilerParams(
            dimension_semantics=("arbitrary",),
            vmem_limit_bytes=VMEM_LIMIT),
        name="layer_final" if final_norm else "layer",
    )(h, h, *params, nf)


def kernel(x, norm_mix, w_in, conv_w, w_pool, pool_scale, a_log, dt_bias, gdn_norm, w_out,
           norm_mlp, w_up, w_down, norm_final):
    depth = w_in.shape[0]
    H = GDN_HEADS
    row = lambda p: p.reshape(depth, 1, p.shape[-1])
    col = lambda p: jnp.pad(p, ((0, 0), (0, H))).reshape(depth, 2 * H, 1)
    params = (
        row(norm_mix),
        w_in,
        jnp.swapaxes(w_in[:, :, MAIN_WIDTH:], 1, 2).astype(BF16),
        conv_w,
        w_pool.astype(BF16),
        row(pool_scale),
        col(a_log),
        col(dt_bias),
        row(gdn_norm),
        w_out,
        row(norm_mlp),
        w_up,
        w_down,
    )
    nf = norm_final.reshape(1, D_MODEL)
    h = x
    for l in range(depth):
        h = _layer(l, h, params, nf, final_norm=(l == depth - 1))
    return h
```

```python
import functools

import jax
import jax.numpy as jnp
from jax import lax
from jax.experimental import pallas as pl
from jax.experimental.pallas import tpu as pltpu

D_MODEL = 1024
CHUNK = 64
POOL_WIDTH = D_MODEL // 2
POOL_GROUPS = 4
POOL_GROUP_DIM = POOL_WIDTH // POOL_GROUPS
POOL_WINDOWS = (2, 4, 8, 16)
GDN_WIDTH = D_MODEL - POOL_WIDTH
GDN_HEADS = 4
GDN_HEAD_DIM = GDN_WIDTH // GDN_HEADS
CONV_WIDTH = 4
D_FF = 4 * D_MODEL
EPS = 1e-6

SUBLANES = 8
MAIN_WIDTH = POOL_WIDTH + 4 * GDN_WIDTH
PROJ_WIDTH = MAIN_WIDTH
QKV_OFF = POOL_WIDTH
Z_OFF = POOL_WIDTH + 3 * GDN_WIDTH
CONV_HALO = SUBLANES
POOL_HALO = 16
INV_BASE = 16

TS = 256
NC = TS // CHUNK
FF_CHUNK = 512
PROJ_CHUNK = 256
WEIGHT_CHUNK_BYTES = 2 * 1024 * 1024
VMEM_LIMIT = 58 * 1024 * 1024

F32 = jnp.float32
BF16 = jnp.bfloat16


def _rms_norm(x, w):
    ms = jnp.mean(x * x, axis=-1, keepdims=True)
    return x * lax.rsqrt(ms + EPS) * w


def _softplus(x):
    return jnp.maximum(x, 0.0) + jnp.log1p(jnp.exp(-jnp.abs(x)))


def _sigmoid(x):
    return 1.0 / (1.0 + jnp.exp(-x))


def _bdot(a, b):
    return jnp.einsum('cik,ckj->cij', a.astype(BF16), b.astype(BF16),
                      preferred_element_type=F32)


def _bdot_nt(a, b):
    return jnp.einsum('cid,cjd->cij', a.astype(BF16), b.astype(BF16),
                      preferred_element_type=F32)


def _unit_lower_inverse(m, fill):
    ri = lax.broadcasted_iota(jnp.int32, m.shape, 1)
    ci = lax.broadcasted_iota(jnp.int32, m.shape, 2)
    eye = (ri == ci).astype(F32)
    d = jnp.where(ri // INV_BASE == ci // INV_BASE, m, 0.0)
    t = eye - d
    p = _bdot(d, d)
    fill()
    power = 2
    while power < INV_BASE:
        t = t + _bdot(t, p)
        power *= 2
        if power < INV_BASE:
            p = _bdot(p, p)
        fill()
    b = INV_BASE
    while b < CHUNK:
        lower_left = ((ri // (2 * b) == ci // (2 * b))
                      & ((ri // b) % 2 == 1) & ((ci // b) % 2 == 0))
        c = jnp.where(lower_left, m, 0.0)
        ct = _bdot(c, t)
        fill()
        t = t - _bdot(t, ct)
        b *= 2
        fill()
    return t


def _inproj_pieces(x_ref, nw_ref, w_ref, wgt_ref, proj_dst, gt_dst, h_dst, u_ref):
    def prepare():
        x = x_ref[...]
        h_dst[...] = x
        u_ref[...] = _rms_norm(x, nw_ref[...]).astype(BF16)

    def project(cols):
        def piece():
            proj_dst[:, cols] = jnp.dot(u_ref[...], w_ref[:, cols], preferred_element_type=F32)
        return piece

    def gates():
        gt = lax.dot_general(wgt_ref[...], u_ref[...], (((1,), (1,)), ((), ())),
                             preferred_element_type=F32)
        for c in range(NC):
            gt_dst[c] = gt[:, c * CHUNK:(c + 1) * CHUNK]

    starts = range(0, PROJ_WIDTH, PROJ_CHUNK)
    return ([prepare] + [project(slice(c0, min(c0 + PROJ_CHUNK, PROJ_WIDTH))) for c0 in starts]
            + [gates])


def _mixer_stage(proj_ref, gt_ref, h_ref, tile_in_row, convw_ref, wpool_ref, pscale_ref,
                 alog_col_ref, dtb_col_ref, gnorm_ref, wout_ref,
                 hmid_dst, state_ref, conv_ref, pool_ref, mix_ref, fill):
    gate_r = gt_ref[...].reshape(NC * 2 * GDN_HEADS, CHUNK)
    alog_col = jnp.concatenate([alog_col_ref[...]] * NC, axis=0)
    dtb_col = jnp.concatenate([dtb_col_ref[...]] * NC, axis=0)
    g_r = -jnp.exp(alog_col) * _softplus(gate_r + dtb_col)
    beta_r = _sigmoid(gate_r)

    conv_ref[CONV_HALO:CONV_HALO + TS, :] = proj_ref[:, QKV_OFF:QKV_OFF + 3 * GDN_WIDTH]
    qkv_blocks = []
    half = TS // 2
    for blk in range(3):
        cs = slice(blk * GDN_WIDTH, (blk + 1) * GDN_WIDTH)
        parts = []
        for r0 in (CONV_HALO, CONV_HALO + half):
            y = conv_ref[r0:r0 + half, cs] * convw_ref[CONV_WIDTH - 1:CONV_WIDTH, cs]
            for j in range(1, CONV_WIDTH):
                y = y + (conv_ref[r0 - j:r0 - j + half, cs]
                         * convw_ref[CONV_WIDTH - 1 - j:CONV_WIDTH - j, cs])
            parts.append(y * _sigmoid(y))
            fill()
        qkv_blocks.append(jnp.concatenate(parts, axis=0))
    conv_ref[0:CONV_HALO, :] = conv_ref[TS:TS + CONV_HALO, :]

    pool_ref[POOL_HALO:POOL_HALO + TS, :] = proj_ref[:, 0:POOL_WIDTH]
    count = (lax.broadcasted_iota(jnp.int32, (TS, POOL_GROUP_DIM), 0)
             + (tile_in_row * TS + 1)).astype(F32)
    for g, w in enumerate(POOL_WINDOWS):
        sl = slice(g * POOL_GROUP_DIM, (g + 1) * POOL_GROUP_DIM)
        x0 = pool_ref[POOL_HALO:POOL_HALO + TS, sl]
        acc = x0
        for j in range(1, w):
            acc = acc + pool_ref[POOL_HALO - j:POOL_HALO - j + TS, sl]
        pooled = acc / jnp.minimum(count, float(w)) - x0
        mix_ref[:, sl] = pooled.astype(BF16)
    pool_ref[0:POOL_HALO, :] = pool_ref[TS:TS + POOL_HALO, :]
    fill()

    def pool_map(g):
        sl = slice(g * POOL_GROUP_DIM, (g + 1) * POOL_GROUP_DIM)
        mixed = jnp.dot(mix_ref[:, sl], wpool_ref[g], preferred_element_type=F32)
        mix_ref[:, sl] = (mixed * pscale_ref[:, sl]).astype(BF16)

    u_i = lax.broadcasted_iota(jnp.int32, (CHUNK, CHUNK), 0)
    v_i = lax.broadcasted_iota(jnp.int32, (CHUNK, CHUNK), 1)
    upper = (u_i <= v_i).astype(F32)
    gcum_r = jnp.dot(g_r, upper, preferred_element_type=F32,
                     precision=lax.Precision.HIGHEST)
    gates_t = jnp.concatenate([gcum_r, beta_r], axis=1).T
    gcum_r = gcum_r.reshape(NC, 2 * GDN_HEADS, CHUNK)

    H = GDN_HEADS
    dh = GDN_HEAD_DIM

    def heads(x2d):
        return jnp.stack([x2d[:, hd * dh:(hd + 1) * dh] for hd in range(H)],
                         axis=0).reshape(H * NC, CHUNK, dh)

    def head_cols(row0, off):
        return jnp.stack([gates_t[row0:row0 + CHUNK, c * 2 * H + off + hd:c * 2 * H + off + hd + 1]
                          for hd in range(H) for c in range(NC)], axis=0)

    def head_rows(x3d, off):
        return jnp.stack([x3d[:, off + hd:off + hd + 1, :] for hd in range(H)],
                         axis=0).reshape(H * NC, 1, CHUNK)

    fill()
    q = heads(qkv_blocks[0])
    k = heads(qkv_blocks[1])
    v = heads(qkv_blocks[2])
    qh = q * (lax.rsqrt(jnp.sum(q * q, axis=-1, keepdims=True) + EPS) * (dh ** -0.5))
    kh = k * lax.rsqrt(jnp.sum(k * k, axis=-1, keepdims=True) + EPS)

    gc = head_cols(0, 0)
    bc = head_cols(CHUNK, H)
    gr = head_rows(gcum_r, 0)
    g_last = gc[:, CHUNK - 1:CHUNK, :]
    eg = jnp.exp(gc)

    ri = lax.broadcasted_iota(jnp.int32, (H * NC, CHUNK, CHUNK), 1)
    ci = lax.broadcasted_iota(jnp.int32, (H * NC, CHUNK, CHUNK), 2)
    decay_incl = jnp.exp(jnp.where(ri >= ci, gc - gr, -jnp.inf))
    decay_strict = jnp.where(ri > ci, decay_incl, 0.0)

    kh_b = kh.astype(BF16)
    m = _bdot_nt(kh_b, kh_b) * bc * decay_strict
    fill()
    t_inv = _unit_lower_inverse(m, fill)
    rhs = jnp.concatenate([v * bc, kh * (bc * eg)], axis=-1)
    sol = _bdot(t_inv, rhs)
    fill()
    per_chunk = lambda x: x.reshape((H, NC) + x.shape[1:])
    value = per_chunk(sol[:, :, :dh])
    k_cumdecay = per_chunk(sol[:, :, dh:])
    attn = per_chunk(_bdot_nt(qh, kh_b) * decay_incl)
    q_dec = per_chunk(qh * eg)
    k_dec = per_chunk(kh * jnp.exp(g_last - gc))
    state_decay = per_chunk(jnp.exp(g_last))
    fill()

    state = state_ref[...]
    outs = []
    for c in range(NC):
        lhs = jnp.concatenate([k_cumdecay[:, c], q_dec[:, c]], axis=1)
        r = _bdot(lhs, state)
        fill()
        v_new = (value[:, c] - r[:, :CHUNK]).astype(BF16)
        outs.append(r[:, CHUNK:] + _bdot(attn[:, c], v_new))
        state = state * state_decay[:, c] + jnp.einsum(
            'hcd,hce->hde', k_dec[:, c].astype(BF16), v_new, preferred_element_type=F32)
        if c < POOL_GROUPS:
            pool_map(c)
        fill()
    for g in range(NC, POOL_GROUPS):
        pool_map(g)
    state_ref[...] = state

    o = jnp.concatenate(outs, axis=1)
    o = o * lax.rsqrt(jnp.mean(o * o, axis=-1, keepdims=True) + EPS) * gnorm_ref[...]
    for hd in range(H):
        z = proj_ref[:, Z_OFF + hd * dh:Z_OFF + (hd + 1) * dh]
        mix_ref[:, POOL_WIDTH + hd * dh:POOL_WIDTH + (hd + 1) * dh] = (
            o[hd] * (z * _sigmoid(z))).astype(BF16)

    hmid_dst[...] = h_ref[...] + jnp.dot(mix_ref[...], wout_ref[...],
                                         preferred_element_type=F32)


def _mlp_pieces(x_ref, nw_ref, wup_ref, wdown_ref, nf_ref, out_ref, u_ref, hid_ref, acc_ref,
                final_norm):
    n = D_FF // FF_CHUNK

    def prepare():
        x = x_ref[...]
        acc_ref[...] = x
        u_ref[...] = _rms_norm(x, nw_ref[...]).astype(BF16)

    def up(j):
        def piece():
            cols = slice(j * FF_CHUNK, (j + 1) * FF_CHUNK)
            hid = jnp.dot(u_ref[...], wup_ref[:, cols], preferred_element_type=F32)
            hid_ref[j % 2] = jnp.square(jnp.maximum(hid, 0.0)).astype(BF16)
        return piece

    def down(j):
        def piece():
            rows = slice(j * FF_CHUNK, (j + 1) * FF_CHUNK)
            acc_ref[...] += jnp.dot(hid_ref[j % 2], wdown_ref[rows, :],
                                    preferred_element_type=F32)
        return piece

    def finish():
        acc = acc_ref[...]
        out_ref[0] = _rms_norm(acc, nf_ref[...]) if final_norm else acc

    order = [prepare, up(0)]
    for j in range(1, n):
        order += [up(j), down(j - 1)]
    return order + [down(n - 1), finish]


def _load_weight(w_hbm, layer, dst):
    n_rows, n_cols = dst.shape
    rows_per_chunk = 1 << ((WEIGHT_CHUNK_BYTES // (4 * n_cols)).bit_length() - 1)
    assert n_rows % rows_per_chunk == 0 and rows_per_chunk % SUBLANES == 0
    n_chunks = n_rows // rows_per_chunk

    def scoped(stage, sem):
        def copy(i):
            return pltpu.make_async_copy(
                w_hbm.at[layer, pl.ds(i * rows_per_chunk, rows_per_chunk), pl.ds(0, n_cols)],
                stage.at[i % 2], sem.at[i % 2])

        copy(0).start()
        for i in range(n_chunks):
            if i + 1 < n_chunks:
                copy(i + 1).start()
            copy(i).wait()
            dst[i * rows_per_chunk:(i + 1) * rows_per_chunk, :] = stage[i % 2].astype(BF16)

    pl.run_scoped(scoped, pltpu.VMEM((2, rows_per_chunk, n_cols), F32),
                  pltpu.SemaphoreType.DMA((2,)))


def _layer_kernel(h_next_ref, h_first_ref, nwmix_ref, win_ref, wgt_ref, convw_ref, wpool_ref,
                  pscale_ref, alog_col_ref, dtb_col_ref, gnorm_ref,
                  wout_hbm, nwmlp_ref, wup_hbm, wdown_hbm, nf_ref, out_ref,
                  proj_scr, gt_scr, h_scr, hmid_scr, state_ref, conv_ref, pool_ref, mix_ref,
                  uin_scr, umlp_scr, hid_scr, acc_scr, wout_ref, wup_ref, wdown_ref,
                  *, layer, tiles_per_row, final_norm):
    s = pl.program_id(0)
    cur = s % 2
    nxt = 1 - cur

    @pl.when(s == 0)
    def _():
        for w_hbm, dst in ((wout_hbm, wout_ref), (wup_hbm, wup_ref), (wdown_hbm, wdown_ref)):
            _load_weight(w_hbm, layer, dst)
        for piece in _inproj_pieces(h_first_ref.at[0], nwmix_ref, win_ref, wgt_ref,
                                    proj_scr.at[0], gt_scr.at[0], h_scr.at[0], uin_scr):
            piece()
        hmid_scr[1] = jnp.zeros((TS, D_MODEL), F32)

    @pl.when(s % tiles_per_row == 0)
    def _():
        state_ref[...] = jnp.zeros_like(state_ref)
        conv_ref[0:CONV_HALO, :] = jnp.zeros((CONV_HALO, 3 * GDN_WIDTH), F32)
        pool_ref[0:POOL_HALO, :] = jnp.zeros((POOL_HALO, POOL_WIDTH), F32)

    mlp = _mlp_pieces(hmid_scr.at[nxt], nwmlp_ref, wup_ref, wdown_ref, nf_ref, out_ref,
                      umlp_scr, hid_scr, acc_scr, final_norm)
    inp = _inproj_pieces(h_next_ref.at[0], nwmix_ref, win_ref, wgt_ref,
                         proj_scr.at[nxt], gt_scr.at[nxt], h_scr.at[nxt], uin_scr)
    dense = iter(mlp[:1] + inp[:1] + mlp[1:-1] + inp[1:-1] + mlp[-1:] + inp[-1:])

    def fill():
        piece = next(dense, None)
        if piece is not None:
            piece()

    fill()
    fill()
    _mixer_stage(proj_scr.at[cur], gt_scr.at[cur], h_scr.at[cur], s % tiles_per_row,
                 convw_ref, wpool_ref, pscale_ref, alog_col_ref,
                 dtb_col_ref, gnorm_ref, wout_ref, hmid_scr.at[cur],
                 state_ref, conv_ref, pool_ref, mix_ref, fill)
    for piece in dense:
        piece()


def _layer(layer, h, params, nf, final_norm):
    B, S, D = h.shape
    tiles_per_row = S // TS
    n_tiles = B * tiles_per_row

    def tile_map(offset):
        def index_map(s):
            t = jnp.clip(s + offset, 0, n_tiles - 1)
            return (t // tiles_per_row, t % tiles_per_row, 0)
        return index_map

    def layer_spec(p):
        block = tuple(p.shape[1:])
        zeros = (0,) * len(block)
        return pl.BlockSpec((None,) + block, lambda s: (layer,) + zeros,
                            pipeline_mode=pl.Buffered(1))

    in_hbm = (9, 11, 12)
    param_specs = [pl.BlockSpec(memory_space=pl.ANY) if i in in_hbm else layer_spec(p)
                   for i, p in enumerate(params)]

    return pl.pallas_call(
        functools.partial(_layer_kernel, layer=layer, tiles_per_row=tiles_per_row,
                          final_norm=final_norm),
        grid=(n_tiles + 1,),
        in_specs=[
            pl.BlockSpec((1, TS, D), tile_map(1)),
            pl.BlockSpec((1, TS, D), lambda s: (0, 0, 0), pipeline_mode=pl.Buffered(1)),
            *param_specs,
            pl.BlockSpec((1, D), lambda s: (0, 0), pipeline_mode=pl.Buffered(1)),
        ],
        out_specs=pl.BlockSpec((1, TS, D), tile_map(-1)),
        out_shape=jax.ShapeDtypeStruct((B, S, D), F32),
        scratch_shapes=[
            pltpu.VMEM((2, TS, PROJ_WIDTH), F32),
            pltpu.VMEM((2, NC, 2 * GDN_HEADS, CHUNK), F32),
            pltpu.VMEM((2, TS, D), F32),
            pltpu.VMEM((2, TS, D), F32),
            pltpu.VMEM((GDN_HEADS, GDN_HEAD_DIM, GDN_HEAD_DIM), F32),
            pltpu.VMEM((TS + CONV_HALO, 3 * GDN_WIDTH), F32),
            pltpu.VMEM((TS + POOL_HALO, POOL_WIDTH), F32),
            pltpu.VMEM((TS, D), BF16),
            pltpu.VMEM((TS, D), BF16),
            pltpu.VMEM((TS, D), BF16),
            pltpu.VMEM((2, TS, FF_CHUNK), BF16),
            pltpu.VMEM((TS, D), F32),
            pltpu.VMEM((D, D), BF16),
            pltpu.VMEM((D, D_FF), BF16),
            pltpu.VMEM((D_FF, D), BF16),
        ],
        compiler_params=pltpu.CompilerParams(
            dimension_semantics=("arbitrary",),
            vmem_limit_bytes=VMEM_LIMIT),
        name="layer_final" if final_norm else "layer",
    )(h, h, *params, nf)


def kernel(x, norm_mix, w_in, conv_w, w_pool, pool_scale, a_log, dt_bias, gdn_norm, w_out,
           norm_mlp, w_up, w_down, norm_final):
    depth = w_in.shape[0]
    H = GDN_HEADS
    row = lambda p: p.reshape(depth, 1, p.shape[-1])
    col = lambda p: jnp.pad(p, ((0, 0), (0, H))).reshape(depth, 2 * H, 1)
    params = (
        row(norm_mix),
        w_in[:, :, :MAIN_WIDTH].astype(BF16),
        jnp.swapaxes(w_in[:, :, MAIN_WIDTH:], 1, 2).astype(BF16),
        conv_w,
        w_pool.astype(BF16),
        row(pool_scale),
        col(a_log),
        col(dt_bias),
        row(gdn_norm),
        w_out,
        row(norm_mlp),
        w_up,
        w_down,
    )
    nf = norm_final.reshape(1, D_MODEL)
    h = x
    for l in range(depth):
        h = _layer(l, h, params, nf, final_norm=(l == depth - 1))
    return h
```

```python
import functools

import jax
import jax.numpy as jnp
from jax import lax
from jax.experimental import pallas as pl
from jax.experimental.pallas import tpu as pltpu

D_MODEL = 1024
CHUNK = 64
POOL_WIDTH = D_MODEL // 2
POOL_GROUPS = 4
POOL_GROUP_DIM = POOL_WIDTH // POOL_GROUPS
POOL_WINDOWS = (2, 4, 8, 16)
GDN_WIDTH = D_MODEL - POOL_WIDTH
GDN_HEADS = 4
GDN_HEAD_DIM = GDN_WIDTH // GDN_HEADS
CONV_WIDTH = 4
D_FF = 4 * D_MODEL
EPS = 1e-6

SUBLANES = 8
MAIN_WIDTH = POOL_WIDTH + 4 * GDN_WIDTH
PROJ_WIDTH = MAIN_WIDTH
QKV_OFF = POOL_WIDTH
Z_OFF = POOL_WIDTH + 3 * GDN_WIDTH
CONV_HALO = SUBLANES
POOL_HALO = 16
INV_BASE = 16

TS = 256
NC = TS // CHUNK
FF_CHUNK = 512
PROJ_CHUNK = 256
WEIGHT_CHUNK_BYTES = 2 * 1024 * 1024
VMEM_LIMIT = 58 * 1024 * 1024

F32 = jnp.float32
BF16 = jnp.bfloat16


def _rms_norm(x, w):
    ms = jnp.mean(x * x, axis=-1, keepdims=True)
    return x * lax.rsqrt(ms + EPS) * w


def _softplus(x):
    return jnp.maximum(x, 0.0) + jnp.log1p(jnp.exp(-jnp.abs(x)))


def _sigmoid(x):
    return 1.0 / (1.0 + jnp.exp(-x))


def _bdot(a, b):
    return jnp.einsum('cik,ckj->cij', a.astype(BF16), b.astype(BF16),
                      preferred_element_type=F32)


def _bdot_nt(a, b):
    return jnp.einsum('cid,cjd->cij', a.astype(BF16), b.astype(BF16),
                      preferred_element_type=F32)


def _unit_lower_inverse(m, fill):
    ri = lax.broadcasted_iota(jnp.int32, m.shape, 1)
    ci = lax.broadcasted_iota(jnp.int32, m.shape, 2)
    eye = (ri == ci).astype(F32)
    d = jnp.where(ri // INV_BASE == ci // INV_BASE, m, 0.0)
    t = eye - d
    p = _bdot(d, d)
    fill()
    power = 2
    while power < INV_BASE:
        t = t + _bdot(t, p)
        power *= 2
        if power < INV_BASE:
            p = _bdot(p, p)
        fill()
    b = INV_BASE
    while b < CHUNK:
        lower_left = ((ri // (2 * b) == ci // (2 * b))
                      & ((ri // b) % 2 == 1) & ((ci // b) % 2 == 0))
        c = jnp.where(lower_left, m, 0.0)
        ct = _bdot(c, t)
        fill()
        t = t - _bdot(t, ct)
        b *= 2
        fill()
    return t


def _inproj_pieces(x_ref, nw_ref, w_ref, wgt_ref, proj_dst, gt_dst, h_dst, u_ref):
    nt = (((1,), (1,)), ((), ()))

    def prepare():
        x = x_ref[...]
        h_dst[...] = x
        u_ref[...] = _rms_norm(x, nw_ref[...]).astype(BF16)

    def project(cols):
        def piece():
            proj_dst[:, cols] = jnp.dot(u_ref[...], w_ref[:, cols], preferred_element_type=F32)
        return piece

    def gates():
        gt = lax.dot_general(wgt_ref[...], u_ref[...], nt,
                             preferred_element_type=F32)
        for c in range(NC):
            gt_dst[c] = gt[:, c * CHUNK:(c + 1) * CHUNK]

    starts = range(0, PROJ_WIDTH, PROJ_CHUNK)
    return ([prepare] + [project(slice(c0, min(c0 + PROJ_CHUNK, PROJ_WIDTH))) for c0 in starts]
            + [gates])


def _mixer_stage(proj_ref, gt_ref, h_ref, tile_in_row, convw_ref, wpool_ref, pscale_ref,
                 alog_col_ref, dtb_col_ref, gnorm_ref, wout_ref,
                 hmid_dst, state_ref, conv_ref, pool_ref, mix_ref, fill):
    gate_r = gt_ref[...].reshape(NC * 2 * GDN_HEADS, CHUNK)
    alog_col = jnp.concatenate([alog_col_ref[...]] * NC, axis=0)
    dtb_col = jnp.concatenate([dtb_col_ref[...]] * NC, axis=0)
    g_r = -jnp.exp(alog_col) * _softplus(gate_r + dtb_col)
    beta_r = _sigmoid(gate_r)

    conv_ref[CONV_HALO:CONV_HALO + TS, :] = proj_ref[:, QKV_OFF:QKV_OFF + 3 * GDN_WIDTH]
    qkv_blocks = []
    half = TS // 2
    for blk in range(3):
        cs = slice(blk * GDN_WIDTH, (blk + 1) * GDN_WIDTH)
        parts = []
        for r0 in (CONV_HALO, CONV_HALO + half):
            y = conv_ref[r0:r0 + half, cs] * convw_ref[CONV_WIDTH - 1:CONV_WIDTH, cs]
            for j in range(1, CONV_WIDTH):
                y = y + (conv_ref[r0 - j:r0 - j + half, cs]
                         * convw_ref[CONV_WIDTH - 1 - j:CONV_WIDTH - j, cs])
            parts.append(y * _sigmoid(y))
            fill()
        qkv_blocks.append(jnp.concatenate(parts, axis=0))
    conv_ref[0:CONV_HALO, :] = conv_ref[TS:TS + CONV_HALO, :]

    pool_ref[POOL_HALO:POOL_HALO + TS, :] = proj_ref[:, 0:POOL_WIDTH]
    count = (lax.broadcasted_iota(jnp.int32, (TS, POOL_GROUP_DIM), 0)
             + (tile_in_row * TS + 1)).astype(F32)
    for g, w in enumerate(POOL_WINDOWS):
        sl = slice(g * POOL_GROUP_DIM, (g + 1) * POOL_GROUP_DIM)
        x0 = pool_ref[POOL_HALO:POOL_HALO + TS, sl]
        acc = x0
        for j in range(1, w):
            acc = acc + pool_ref[POOL_HALO - j:POOL_HALO - j + TS, sl]
        pooled = acc / jnp.minimum(count, float(w)) - x0
        mix_ref[:, sl] = pooled.astype(BF16)
    pool_ref[0:POOL_HALO, :] = pool_ref[TS:TS + POOL_HALO, :]
    fill()

    def pool_map(g):
        sl = slice(g * POOL_GROUP_DIM, (g + 1) * POOL_GROUP_DIM)
        mixed = jnp.dot(mix_ref[:, sl], wpool_ref[g], preferred_element_type=F32)
        mix_ref[:, sl] = (mixed * pscale_ref[:, sl]).astype(BF16)

    u_i = lax.broadcasted_iota(jnp.int32, (CHUNK, CHUNK), 0)
    v_i = lax.broadcasted_iota(jnp.int32, (CHUNK, CHUNK), 1)
    upper = (u_i <= v_i).astype(F32)
    gcum_r = jnp.dot(g_r, upper, preferred_element_type=F32,
                     precision=lax.Precision.HIGHEST)
    gates_t = jnp.concatenate([gcum_r, beta_r], axis=1).T
    gcum_r = gcum_r.reshape(NC, 2 * GDN_HEADS, CHUNK)

    H = GDN_HEADS
    dh = GDN_HEAD_DIM

    def heads(x2d):
        return jnp.stack([x2d[:, hd * dh:(hd + 1) * dh] for hd in range(H)],
                         axis=0).reshape(H * NC, CHUNK, dh)

    def head_cols(row0, off):
        return jnp.stack([gates_t[row0:row0 + CHUNK, c * 2 * H + off + hd:c * 2 * H + off + hd + 1]
                          for hd in range(H) for c in range(NC)], axis=0)

    def head_rows(x3d, off):
        return jnp.stack([x3d[:, off + hd:off + hd + 1, :] for hd in range(H)],
                         axis=0).reshape(H * NC, 1, CHUNK)

    fill()
    q = heads(qkv_blocks[0])
    k = heads(qkv_blocks[1])
    v = heads(qkv_blocks[2])
    qh = q * (lax.rsqrt(jnp.sum(q * q, axis=-1, keepdims=True) + EPS) * (dh ** -0.5))
    kh = k * lax.rsqrt(jnp.sum(k * k, axis=-1, keepdims=True) + EPS)

    gc = head_cols(0, 0)
    bc = head_cols(CHUNK, H)
    gr = head_rows(gcum_r, 0)
    g_last = gc[:, CHUNK - 1:CHUNK, :]
    eg = jnp.exp(gc)

    ri = lax.broadcasted_iota(jnp.int32, (H * NC, CHUNK, CHUNK), 1)
    ci = lax.broadcasted_iota(jnp.int32, (H * NC, CHUNK, CHUNK), 2)
    decay_incl = jnp.exp(jnp.where(ri >= ci, gc - gr, -jnp.inf))
    decay_strict = jnp.where(ri > ci, decay_incl, 0.0)

    kh_b = kh.astype(BF16)
    m = _bdot_nt(kh_b, kh_b) * bc * decay_strict
    fill()
    t_inv = _unit_lower_inverse(m, fill)
    rhs = jnp.concatenate([v * bc, kh * (bc * eg)], axis=-1)
    sol = _bdot(t_inv, rhs)
    fill()
    per_chunk = lambda x: x.reshape((H, NC) + x.shape[1:])
    value = per_chunk(sol[:, :, :dh])
    k_cumdecay = per_chunk(sol[:, :, dh:])
    attn = per_chunk(_bdot_nt(qh, kh_b) * decay_incl)
    q_dec = per_chunk(qh * eg)
    k_dec = per_chunk(kh * jnp.exp(g_last - gc))
    state_decay = per_chunk(jnp.exp(g_last))
    fill()

    state = state_ref[...]
    outs = []
    for c in range(NC):
        lhs = jnp.concatenate([k_cumdecay[:, c], q_dec[:, c]], axis=1)
        r = _bdot(lhs, state)
        fill()
        v_new = (value[:, c] - r[:, :CHUNK]).astype(BF16)
        outs.append(r[:, CHUNK:] + _bdot(attn[:, c], v_new))
        state = state * state_decay[:, c] + jnp.einsum(
            'hcd,hce->hde', k_dec[:, c].astype(BF16), v_new, preferred_element_type=F32)
        if c < POOL_GROUPS:
            pool_map(c)
        fill()
    for g in range(NC, POOL_GROUPS):
        pool_map(g)
    state_ref[...] = state

    o = jnp.concatenate(outs, axis=1)
    o = o * lax.rsqrt(jnp.mean(o * o, axis=-1, keepdims=True) + EPS) * gnorm_ref[...]
    for hd in range(H):
        z = proj_ref[:, Z_OFF + hd * dh:Z_OFF + (hd + 1) * dh]
        mix_ref[:, POOL_WIDTH + hd * dh:POOL_WIDTH + (hd + 1) * dh] = (
            o[hd] * (z * _sigmoid(z))).astype(BF16)

    hmid_dst[...] = h_ref[...] + jnp.dot(mix_ref[...], wout_ref[...],
                                         preferred_element_type=F32)


def _mlp_pieces(x_ref, nw_ref, wup_ref, wdown_ref, nf_ref, out_ref, u_ref, hid_ref, acc_ref,
                final_norm):
    n = D_FF // FF_CHUNK

    def prepare():
        x = x_ref[...]
        acc_ref[...] = x
        u_ref[...] = _rms_norm(x, nw_ref[...]).astype(BF16)

    def up(j):
        def piece():
            cols = slice(j * FF_CHUNK, (j + 1) * FF_CHUNK)
            hid = jnp.dot(u_ref[...], wup_ref[:, cols], preferred_element_type=F32)
            hid_ref[j % 2] = jnp.square(jnp.maximum(hid, 0.0)).astype(BF16)
        return piece

    def down(j):
        def piece():
            rows = slice(j * FF_CHUNK, (j + 1) * FF_CHUNK)
            acc_ref[...] += jnp.dot(hid_ref[j % 2], wdown_ref[rows, :],
                                    preferred_element_type=F32)
        return piece

    def finish():
        acc = acc_ref[...]
        out_ref[0] = _rms_norm(acc, nf_ref[...]) if final_norm else acc

    order = [prepare, up(0)]
    for j in range(1, n):
        order += [up(j), down(j - 1)]
    return order + [down(n - 1), finish]


def _load_weight(w_hbm, layer, dst):
    n_rows, n_cols = dst.shape
    rows_per_chunk = 1 << ((WEIGHT_CHUNK_BYTES // (4 * n_cols)).bit_length() - 1)
    assert n_rows % rows_per_chunk == 0 and rows_per_chunk % SUBLANES == 0
    n_chunks = n_rows // rows_per_chunk

    def scoped(stage, sem):
        def copy(i):
            return pltpu.make_async_copy(
                w_hbm.at[layer, pl.ds(i * rows_per_chunk, rows_per_chunk), pl.ds(0, n_cols)],
                stage.at[i % 2], sem.at[i % 2])

        copy(0).start()
        for i in range(n_chunks):
            if i + 1 < n_chunks:
                copy(i + 1).start()
            copy(i).wait()
            dst[i * rows_per_chunk:(i + 1) * rows_per_chunk, :] = stage[i % 2].astype(BF16)

    pl.run_scoped(scoped, pltpu.VMEM((2, rows_per_chunk, n_cols), F32),
                  pltpu.SemaphoreType.DMA((2,)))


def _load_weight_transposed(wt_hbm, layer, dst):
    n_in, n_out = dst.shape
    rows = PROJ_CHUNK
    n_chunks = n_out // rows

    def scoped(stage, sem):
        def copy(i):
            return pltpu.make_async_copy(
                wt_hbm.at[layer, pl.ds(i * rows, rows), pl.ds(0, n_in)],
                stage.at[i % 2], sem.at[i % 2])

        copy(0).start()
        for i in range(n_chunks):
            if i + 1 < n_chunks:
                copy(i + 1).start()
            copy(i).wait()
            dst[:, i * rows:(i + 1) * rows] = stage[i % 2].T.astype(BF16)

    pl.run_scoped(scoped, pltpu.VMEM((2, rows, n_in), F32), pltpu.SemaphoreType.DMA((2,)))


def _layer_kernel(h_next_ref, h_first_ref, nwmix_ref, wint_hbm, wgt_ref, convw_ref, wpool_ref,
                  pscale_ref, alog_col_ref, dtb_col_ref, gnorm_ref,
                  wout_hbm, nwmlp_ref, wup_hbm, wdown_hbm, nf_ref, out_ref,
                  proj_scr, gt_scr, h_scr, hmid_scr, state_ref, conv_ref, pool_ref, mix_ref,
                  uin_scr, umlp_scr, hid_scr, acc_scr, win_ref, wout_ref, wup_ref, wdown_ref,
                  *, layer, tiles_per_row, final_norm):
    s = pl.program_id(0)
    cur = s % 2
    nxt = 1 - cur

    @pl.when(s == 0)
    def _():
        _load_weight_transposed(wint_hbm, layer, win_ref)
        for w_hbm, dst in ((wout_hbm, wout_ref), (wup_hbm, wup_ref), (wdown_hbm, wdown_ref)):
            _load_weight(w_hbm, layer, dst)
        for piece in _inproj_pieces(h_first_ref.at[0], nwmix_ref, win_ref, wgt_ref,
                                    proj_scr.at[0], gt_scr.at[0], h_scr.at[0], uin_scr):
            piece()
        hmid_scr[1] = jnp.zeros((TS, D_MODEL), F32)

    @pl.when(s % tiles_per_row == 0)
    def _():
        state_ref[...] = jnp.zeros_like(state_ref)
        conv_ref[0:CONV_HALO, :] = jnp.zeros((CONV_HALO, 3 * GDN_WIDTH), F32)
        pool_ref[0:POOL_HALO, :] = jnp.zeros((POOL_HALO, POOL_WIDTH), F32)

    mlp = _mlp_pieces(hmid_scr.at[nxt], nwmlp_ref, wup_ref, wdown_ref, nf_ref, out_ref,
                      umlp_scr, hid_scr, acc_scr, final_norm)
    inp = _inproj_pieces(h_next_ref.at[0], nwmix_ref, win_ref, wgt_ref,
                         proj_scr.at[nxt], gt_scr.at[nxt], h_scr.at[nxt], uin_scr)
    dense = iter(mlp[:1] + inp[:1] + mlp[1:-1] + inp[1:-1] + mlp[-1:] + inp[-1:])

    def fill():
        piece = next(dense, None)
        if piece is not None:
            piece()

    fill()
    fill()
    _mixer_stage(proj_scr.at[cur], gt_scr.at[cur], h_scr.at[cur], s % tiles_per_row,
                 convw_ref, wpool_ref, pscale_ref, alog_col_ref,
                 dtb_col_ref, gnorm_ref, wout_ref, hmid_scr.at[cur],
                 state_ref, conv_ref, pool_ref, mix_ref, fill)
    for piece in dense:
        piece()


def _layer(layer, h, params, nf, final_norm):
    B, S, D = h.shape
    tiles_per_row = S // TS
    n_tiles = B * tiles_per_row

    def tile_map(offset):
        def index_map(s):
            t = jnp.clip(s + offset, 0, n_tiles - 1)
            return (t // tiles_per_row, t % tiles_per_row, 0)
        return index_map

    def layer_spec(p):
        block = tuple(p.shape[1:])
        zeros = (0,) * len(block)
        return pl.BlockSpec((None,) + block, lambda s: (layer,) + zeros,
                            pipeline_mode=pl.Buffered(1))

    in_hbm = (1, 9, 11, 12)
    param_specs = [pl.BlockSpec(memory_space=pl.ANY) if i in in_hbm else layer_spec(p)
                   for i, p in enumerate(params)]

    return pl.pallas_call(
        functools.partial(_layer_kernel, layer=layer, tiles_per_row=tiles_per_row,
                          final_norm=final_norm),
        grid=(n_tiles + 1,),
        in_specs=[
            pl.BlockSpec((1, TS, D), tile_map(1)),
            pl.BlockSpec((1, TS, D), lambda s: (0, 0, 0), pipeline_mode=pl.Buffered(1)),
            *param_specs,
            pl.BlockSpec((1, D), lambda s: (0, 0), pipeline_mode=pl.Buffered(1)),
        ],
        out_specs=pl.BlockSpec((1, TS, D), tile_map(-1)),
        out_shape=jax.ShapeDtypeStruct((B, S, D), F32),
        scratch_shapes=[
            pltpu.VMEM((2, TS, PROJ_WIDTH), F32),
            pltpu.VMEM((2, NC, 2 * GDN_HEADS, CHUNK), F32),
            pltpu.VMEM((2, TS, D), F32),
            pltpu.VMEM((2, TS, D), F32),
            pltpu.VMEM((GDN_HEADS, GDN_HEAD_DIM, GDN_HEAD_DIM), F32),
            pltpu.VMEM((TS + CONV_HALO, 3 * GDN_WIDTH), F32),
            pltpu.VMEM((TS + POOL_HALO, POOL_WIDTH), F32),
            pltpu.VMEM((TS, D), BF16),
            pltpu.VMEM((TS, D), BF16),
            pltpu.VMEM((TS, D), BF16),
            pltpu.VMEM((2, TS, FF_CHUNK), BF16),
            pltpu.VMEM((TS, D), F32),
            pltpu.VMEM((D, PROJ_WIDTH), BF16),
            pltpu.VMEM((D, D), BF16),
            pltpu.VMEM((D, D_FF), BF16),
            pltpu.VMEM((D_FF, D), BF16),
        ],
        compiler_params=pltpu.CompilerParams(
            dimension_semantics=("arbitrary",),
            vmem_limit_bytes=VMEM_LIMIT),
        name="layer_final" if final_norm else "layer",
    )(h, h, *params, nf)


def kernel(x, norm_mix, w_in, conv_w, w_pool, pool_scale, a_log, dt_bias, gdn_norm, w_out,
           norm_mlp, w_up, w_down, norm_final):
    depth = w_in.shape[0]
    H = GDN_HEADS
    row = lambda p: p.reshape(depth, 1, p.shape[-1])
    col = lambda p: jnp.pad(p, ((0, 0), (0, H))).reshape(depth, 2 * H, 1)
    w_in_t = jnp.swapaxes(w_in, 1, 2)
    params = (
        row(norm_mix),
        w_in_t,
        w_in_t[:, MAIN_WIDTH:, :].astype(BF16),
        conv_w,
        w_pool.astype(BF16),
        row(pool_scale),
        col(a_log),
        col(dt_bias),
        row(gdn_norm),
        w_out,
        row(norm_mlp),
        w_up,
        w_down,
    )
    nf = norm_final.reshape(1, D_MODEL)
    h = x
    for l in range(depth):
        h = _layer(l, h, params, nf, final_norm=(l == depth - 1))
    return h
```

```python
import functools

import jax
import jax.numpy as jnp
from jax import lax
from jax.experimental import pallas as pl
from jax.experimental.pallas import tpu as pltpu

D_MODEL = 1024
CHUNK = 64
POOL_WIDTH = D_MODEL // 2
POOL_GROUPS = 4
POOL_GROUP_DIM = POOL_WIDTH // POOL_GROUPS
POOL_WINDOWS = (2, 4, 8, 16)
GDN_WIDTH = D_MODEL - POOL_WIDTH
GDN_HEADS = 4
GDN_HEAD_DIM = GDN_WIDTH // GDN_HEADS
CONV_WIDTH = 4
D_FF = 4 * D_MODEL
EPS = 1e-6

SUBLANES = 8
MAIN_WIDTH = POOL_WIDTH + 4 * GDN_WIDTH
PROJ_WIDTH = MAIN_WIDTH
QKV_OFF = POOL_WIDTH
Z_OFF = POOL_WIDTH + 3 * GDN_WIDTH
CONV_HALO = SUBLANES
POOL_HALO = 16
INV_BASE = 16

TS = 256
NC = TS // CHUNK
FF_CHUNK = 512
PROJ_CHUNK = 256
WEIGHT_CHUNK_BYTES = 2 * 1024 * 1024
WEIGHT_SLOTS = 4
VMEM_LIMIT = 58 * 1024 * 1024

F32 = jnp.float32
BF16 = jnp.bfloat16


def _rms_norm(x, w):
    ms = jnp.mean(x * x, axis=-1, keepdims=True)
    return x * lax.rsqrt(ms + EPS) * w


def _softplus(x):
    return jnp.maximum(x, 0.0) + jnp.log1p(jnp.exp(-jnp.abs(x)))


def _sigmoid(x):
    return 1.0 / (1.0 + jnp.exp(-x))


def _bdot(a, b):
    return jnp.einsum('cik,ckj->cij', a.astype(BF16), b.astype(BF16),
                      preferred_element_type=F32)


def _bdot_nt(a, b):
    return jnp.einsum('cid,cjd->cij', a.astype(BF16), b.astype(BF16),
                      preferred_element_type=F32)


def _unit_lower_inverse(m, fill):
    ri = lax.broadcasted_iota(jnp.int32, m.shape, 1)
    ci = lax.broadcasted_iota(jnp.int32, m.shape, 2)
    eye = (ri == ci).astype(F32)
    d = jnp.where(ri // INV_BASE == ci // INV_BASE, m, 0.0)
    t = eye - d
    p = _bdot(d, d)
    fill()
    power = 2
    while power < INV_BASE:
        t = t + _bdot(t, p)
        power *= 2
        if power < INV_BASE:
            p = _bdot(p, p)
        fill()
    b = INV_BASE
    while b < CHUNK:
        lower_left = ((ri // (2 * b) == ci // (2 * b))
                      & ((ri // b) % 2 == 1) & ((ci // b) % 2 == 0))
        c = jnp.where(lower_left, m, 0.0)
        ct = _bdot(c, t)
        fill()
        t = t - _bdot(t, ct)
        b *= 2
        fill()
    return t


def _inproj_pieces(x_ref, nw_ref, w_ref, wgt_ref, proj_dst, gt_dst, h_dst, u_ref):
    nt = (((1,), (1,)), ((), ()))

    def prepare():
        x = x_ref[...]
        h_dst[...] = x
        u_ref[...] = _rms_norm(x, nw_ref[...]).astype(BF16)

    def project(cols):
        def piece():
            proj_dst[:, cols] = jnp.dot(u_ref[...], w_ref[:, cols], preferred_element_type=F32)
        return piece

    def gates():
        gt = lax.dot_general(wgt_ref[...], u_ref[...], nt,
                             preferred_element_type=F32)
        for c in range(NC):
            gt_dst[c] = gt[:, c * CHUNK:(c + 1) * CHUNK]

    starts = range(0, PROJ_WIDTH, PROJ_CHUNK)
    return ([prepare] + [project(slice(c0, min(c0 + PROJ_CHUNK, PROJ_WIDTH))) for c0 in starts]
            + [gates])


def _mixer_stage(proj_ref, gt_ref, h_ref, tile_in_row, convw_ref, wpool_ref, pscale_ref,
                 alog_col_ref, dtb_col_ref, gnorm_ref, wout_ref,
                 hmid_dst, state_ref, conv_ref, pool_ref, mix_ref, fill):
    gate_r = gt_ref[...].reshape(NC * 2 * GDN_HEADS, CHUNK)
    alog_col = jnp.concatenate([alog_col_ref[...]] * NC, axis=0)
    dtb_col = jnp.concatenate([dtb_col_ref[...]] * NC, axis=0)
    g_r = -jnp.exp(alog_col) * _softplus(gate_r + dtb_col)
    beta_r = _sigmoid(gate_r)

    conv_ref[CONV_HALO:CONV_HALO + TS, :] = proj_ref[:, QKV_OFF:QKV_OFF + 3 * GDN_WIDTH]
    qkv_blocks = []
    half = TS // 2
    for blk in range(3):
        cs = slice(blk * GDN_WIDTH, (blk + 1) * GDN_WIDTH)
        parts = []
        for r0 in (CONV_HALO, CONV_HALO + half):
            y = conv_ref[r0:r0 + half, cs] * convw_ref[CONV_WIDTH - 1:CONV_WIDTH, cs]
            for j in range(1, CONV_WIDTH):
                y = y + (conv_ref[r0 - j:r0 - j + half, cs]
                         * convw_ref[CONV_WIDTH - 1 - j:CONV_WIDTH - j, cs])
            parts.append(y * _sigmoid(y))
            fill()
        qkv_blocks.append(jnp.concatenate(parts, axis=0))
    conv_ref[0:CONV_HALO, :] = conv_ref[TS:TS + CONV_HALO, :]

    pool_ref[POOL_HALO:POOL_HALO + TS, :] = proj_ref[:, 0:POOL_WIDTH]
    count = (lax.broadcasted_iota(jnp.int32, (TS, POOL_GROUP_DIM), 0)
             + (tile_in_row * TS + 1)).astype(F32)
    for g, w in enumerate(POOL_WINDOWS):
        sl = slice(g * POOL_GROUP_DIM, (g + 1) * POOL_GROUP_DIM)
        x0 = pool_ref[POOL_HALO:POOL_HALO + TS, sl]
        acc = x0
        for j in range(1, w):
            acc = acc + pool_ref[POOL_HALO - j:POOL_HALO - j + TS, sl]
        pooled = acc / jnp.minimum(count, float(w)) - x0
        mix_ref[:, sl] = pooled.astype(BF16)
    pool_ref[0:POOL_HALO, :] = pool_ref[TS:TS + POOL_HALO, :]
    fill()

    def pool_map(g):
        sl = slice(g * POOL_GROUP_DIM, (g + 1) * POOL_GROUP_DIM)
        mixed = jnp.dot(mix_ref[:, sl], wpool_ref[g], preferred_element_type=F32)
        mix_ref[:, sl] = (mixed * pscale_ref[:, sl]).astype(BF16)

    u_i = lax.broadcasted_iota(jnp.int32, (CHUNK, CHUNK), 0)
    v_i = lax.broadcasted_iota(jnp.int32, (CHUNK, CHUNK), 1)
    upper = (u_i <= v_i).astype(F32)
    gcum_r = jnp.dot(g_r, upper, preferred_element_type=F32,
                     precision=lax.Precision.HIGHEST)
    gates_t = jnp.concatenate([gcum_r, beta_r], axis=1).T
    gcum_r = gcum_r.reshape(NC, 2 * GDN_HEADS, CHUNK)

    H = GDN_HEADS
    dh = GDN_HEAD_DIM

    def heads(x2d):
        return jnp.stack([x2d[:, hd * dh:(hd + 1) * dh] for hd in range(H)],
                         axis=0).reshape(H * NC, CHUNK, dh)

    def head_cols(row0, off):
        return jnp.stack([gates_t[row0:row0 + CHUNK, c * 2 * H + off + hd:c * 2 * H + off + hd + 1]
                          for hd in range(H) for c in range(NC)], axis=0)

    def head_rows(x3d, off):
        return jnp.stack([x3d[:, off + hd:off + hd + 1, :] for hd in range(H)],
                         axis=0).reshape(H * NC, 1, CHUNK)

    fill()
    q = heads(qkv_blocks[0])
    k = heads(qkv_blocks[1])
    v = heads(qkv_blocks[2])
    qh = q * (lax.rsqrt(jnp.sum(q * q, axis=-1, keepdims=True) + EPS) * (dh ** -0.5))
    kh = k * lax.rsqrt(jnp.sum(k * k, axis=-1, keepdims=True) + EPS)

    gc = head_cols(0, 0)
    bc = head_cols(CHUNK, H)
    gr = head_rows(gcum_r, 0)
    g_last = gc[:, CHUNK - 1:CHUNK, :]
    eg = jnp.exp(gc)

    ri = lax.broadcasted_iota(jnp.int32, (H * NC, CHUNK, CHUNK), 1)
    ci = lax.broadcasted_iota(jnp.int32, (H * NC, CHUNK, CHUNK), 2)
    decay_incl = jnp.exp(jnp.where(ri >= ci, gc - gr, -jnp.inf))
    decay_strict = jnp.where(ri > ci, decay_incl, 0.0)

    kh_b = kh.astype(BF16)
    m = _bdot_nt(kh_b, kh_b) * bc * decay_strict
    fill()
    t_inv = _unit_lower_inverse(m, fill)
    rhs = jnp.concatenate([v * bc, kh * (bc * eg)], axis=-1)
    sol = _bdot(t_inv, rhs)
    fill()
    per_chunk = lambda x: x.reshape((H, NC) + x.shape[1:])
    value = per_chunk(sol[:, :, :dh])
    k_cumdecay = per_chunk(sol[:, :, dh:])
    attn = per_chunk(_bdot_nt(qh, kh_b) * decay_incl)
    q_dec = per_chunk(qh * eg)
    k_dec = per_chunk(kh * jnp.exp(g_last - gc))
    state_decay = per_chunk(jnp.exp(g_last))
    fill()

    state = state_ref[...]
    outs = []
    for c in range(NC):
        lhs = jnp.concatenate([k_cumdecay[:, c], q_dec[:, c]], axis=1)
        r = _bdot(lhs, state)
        fill()
        v_new = (value[:, c] - r[:, :CHUNK]).astype(BF16)
        outs.append(r[:, CHUNK:] + _bdot(attn[:, c], v_new))
        state = state * state_decay[:, c] + jnp.einsum(
            'hcd,hce->hde', k_dec[:, c].astype(BF16), v_new, preferred_element_type=F32)
        if c < POOL_GROUPS:
            pool_map(c)
        fill()
    for g in range(NC, POOL_GROUPS):
        pool_map(g)
    state_ref[...] = state

    o = jnp.concatenate(outs, axis=1)
    o = o * lax.rsqrt(jnp.mean(o * o, axis=-1, keepdims=True) + EPS) * gnorm_ref[...]
    for hd in range(H):
        z = proj_ref[:, Z_OFF + hd * dh:Z_OFF + (hd + 1) * dh]
        mix_ref[:, POOL_WIDTH + hd * dh:POOL_WIDTH + (hd + 1) * dh] = (
            o[hd] * (z * _sigmoid(z))).astype(BF16)

    hmid_dst[...] = h_ref[...] + jnp.dot(mix_ref[...], wout_ref[...],
                                         preferred_element_type=F32)


def _mlp_pieces(x_ref, nw_ref, wup_ref, wdown_ref, nf_ref, out_ref, u_ref, hid_ref, acc_ref,
                final_norm):
    n = D_FF // FF_CHUNK

    def prepare():
        x = x_ref[...]
        acc_ref[...] = x
        u_ref[...] = _rms_norm(x, nw_ref[...]).astype(BF16)

    def up(j):
        def piece():
            cols = slice(j * FF_CHUNK, (j + 1) * FF_CHUNK)
            hid = jnp.dot(u_ref[...], wup_ref[:, cols], preferred_element_type=F32)
            hid_ref[j % 2] = jnp.square(jnp.maximum(hid, 0.0)).astype(BF16)
        return piece

    def down(j):
        def piece():
            rows = slice(j * FF_CHUNK, (j + 1) * FF_CHUNK)
            acc_ref[...] += jnp.dot(hid_ref[j % 2], wdown_ref[rows, :],
                                    preferred_element_type=F32)
        return piece

    def finish():
        acc = acc_ref[...]
        out_ref[0] = _rms_norm(acc, nf_ref[...]) if final_norm else acc

    order = [prepare, up(0)]
    for j in range(1, n):
        order += [up(j), down(j - 1)]
    return order + [down(n - 1), finish]


def _stream_weight(w_hbm, layer, rows_per_chunk, n_chunks, n_cols, consume):
    def scoped(stage, sem):
        def copy(i):
            return pltpu.make_async_copy(
                w_hbm.at[layer, pl.ds(i * rows_per_chunk, rows_per_chunk), pl.ds(0, n_cols)],
                stage.at[i % WEIGHT_SLOTS], sem.at[i % WEIGHT_SLOTS])

        for i in range(min(WEIGHT_SLOTS - 1, n_chunks)):
            copy(i).start()
        for i in range(n_chunks):
            if i + WEIGHT_SLOTS - 1 < n_chunks:
                copy(i + WEIGHT_SLOTS - 1).start()
            copy(i).wait()
            consume(i, stage.at[i % WEIGHT_SLOTS])

    pl.run_scoped(scoped, pltpu.VMEM((WEIGHT_SLOTS, rows_per_chunk, n_cols), F32),
                  pltpu.SemaphoreType.DMA((WEIGHT_SLOTS,)))


def _load_weight(w_hbm, layer, dst):
    n_rows, n_cols = dst.shape
    rows = 1 << ((WEIGHT_CHUNK_BYTES // (4 * n_cols)).bit_length() - 1)
    assert n_rows % rows == 0 and rows % SUBLANES == 0

    def consume(i, chunk):
        dst[i * rows:(i + 1) * rows, :] = chunk[...].astype(BF16)

    _stream_weight(w_hbm, layer, rows, n_rows // rows, n_cols, consume)


def _load_weight_transposed(wt_hbm, layer, dst):
    n_in, n_out = dst.shape
    rows = PROJ_CHUNK
    assert n_out % rows == 0

    def consume(i, chunk):
        dst[:, i * rows:(i + 1) * rows] = chunk[...].T.astype(BF16)

    _stream_weight(wt_hbm, layer, rows, n_out // rows, n_in, consume)


def _layer_kernel(h_next_ref, h_first_ref, nwmix_ref, wint_hbm, wgt_ref, convw_ref, wpool_ref,
                  pscale_ref, alog_col_ref, dtb_col_ref, gnorm_ref,
                  wout_hbm, nwmlp_ref, wup_hbm, wdown_hbm, nf_ref, out_ref,
                  proj_scr, gt_scr, h_scr, hmid_scr, state_ref, conv_ref, pool_ref, mix_ref,
                  uin_scr, umlp_scr, hid_scr, acc_scr, win_ref, wout_ref, wup_ref, wdown_ref,
                  *, layer, tiles_per_row, final_norm):
    s = pl.program_id(0)
    cur = s % 2
    nxt = 1 - cur

    @pl.when(s == 0)
    def _():
        _load_weight_transposed(wint_hbm, layer, win_ref)
        for w_hbm, dst in ((wout_hbm, wout_ref), (wup_hbm, wup_ref), (wdown_hbm, wdown_ref)):
            _load_weight(w_hbm, layer, dst)
        for piece in _inproj_pieces(h_first_ref.at[0], nwmix_ref, win_ref, wgt_ref,
                                    proj_scr.at[0], gt_scr.at[0], h_scr.at[0], uin_scr):
            piece()
        hmid_scr[1] = jnp.zeros((TS, D_MODEL), F32)

    @pl.when(s % tiles_per_row == 0)
    def _():
        state_ref[...] = jnp.zeros_like(state_ref)
        conv_ref[0:CONV_HALO, :] = jnp.zeros((CONV_HALO, 3 * GDN_WIDTH), F32)
        pool_ref[0:POOL_HALO, :] = jnp.zeros((POOL_HALO, POOL_WIDTH), F32)

    mlp = _mlp_pieces(hmid_scr.at[nxt], nwmlp_ref, wup_ref, wdown_ref, nf_ref, out_ref,
                      umlp_scr, hid_scr, acc_scr, final_norm)
    inp = _inproj_pieces(h_next_ref.at[0], nwmix_ref, win_ref, wgt_ref,
                         proj_scr.at[nxt], gt_scr.at[nxt], h_scr.at[nxt], uin_scr)
    dense = iter(mlp[:1] + inp[:1] + mlp[1:-1] + inp[1:-1] + mlp[-1:] + inp[-1:])

    def fill():
        piece = next(dense, None)
        if piece is not None:
            piece()

    fill()
    fill()
    _mixer_stage(proj_scr.at[cur], gt_scr.at[cur], h_scr.at[cur], s % tiles_per_row,
                 convw_ref, wpool_ref, pscale_ref, alog_col_ref,
                 dtb_col_ref, gnorm_ref, wout_ref, hmid_scr.at[cur],
                 state_ref, conv_ref, pool_ref, mix_ref, fill)
    for piece in dense:
        piece()


def _layer(layer, h, params, nf, final_norm):
    B, S, D = h.shape
    tiles_per_row = S // TS
    n_tiles = B * tiles_per_row

    def tile_map(offset):
        def index_map(s):
            t = jnp.clip(s + offset, 0, n_tiles - 1)
            return (t // tiles_per_row, t % tiles_per_row, 0)
        return index_map

    def layer_spec(p):
        block = tuple(p.shape[1:])
        zeros = (0,) * len(block)
        return pl.BlockSpec((None,) + block, lambda s: (layer,) + zeros,
                            pipeline_mode=pl.Buffered(1))

    in_hbm = (1, 9, 11, 12)
    param_specs = [pl.BlockSpec(memory_space=pl.ANY) if i in in_hbm else layer_spec(p)
                   for i, p in enumerate(params)]

    return pl.pallas_call(
        functools.partial(_layer_kernel, layer=layer, tiles_per_row=tiles_per_row,
                          final_norm=final_norm),
        grid=(n_tiles + 1,),
        in_specs=[
            pl.BlockSpec((1, TS, D), tile_map(1)),
            pl.BlockSpec((1, TS, D), lambda s: (0, 0, 0), pipeline_mode=pl.Buffered(1)),
            *param_specs,
            pl.BlockSpec((1, D), lambda s: (0, 0), pipeline_mode=pl.Buffered(1)),
        ],
        out_specs=pl.BlockSpec((1, TS, D), tile_map(-1)),
        out_shape=jax.ShapeDtypeStruct((B, S, D), F32),
        scratch_shapes=[
            pltpu.VMEM((2, TS, PROJ_WIDTH), F32),
            pltpu.VMEM((2, NC, 2 * GDN_HEADS, CHUNK), F32),
            pltpu.VMEM((2, TS, D), F32),
            pltpu.VMEM((2, TS, D), F32),
            pltpu.VMEM((GDN_HEADS, GDN_HEAD_DIM, GDN_HEAD_DIM), F32),
            pltpu.VMEM((TS + CONV_HALO, 3 * GDN_WIDTH), F32),
            pltpu.VMEM((TS + POOL_HALO, POOL_WIDTH), F32),
            pltpu.VMEM((TS, D), BF16),
            pltpu.VMEM((TS, D), BF16),
            pltpu.VMEM((TS, D), BF16),
            pltpu.VMEM((2, TS, FF_CHUNK), BF16),
            pltpu.VMEM((TS, D), F32),
            pltpu.VMEM((D, PROJ_WIDTH), BF16),
            pltpu.VMEM((D, D), BF16),
            pltpu.VMEM((D, D_FF), BF16),
            pltpu.VMEM((D_FF, D), BF16),
        ],
        compiler_params=pltpu.CompilerParams(
            dimension_semantics=("arbitrary",),
            vmem_limit_bytes=VMEM_LIMIT),
        name="layer_final" if final_norm else "layer",
    )(h, h, *params, nf)


def kernel(x, norm_mix, w_in, conv_w, w_pool, pool_scale, a_log, dt_bias, gdn_norm, w_out,
           norm_mlp, w_up, w_down, norm_final):
    depth = w_in.shape[0]
    H = GDN_HEADS
    row = lambda p: p.reshape(depth, 1, p.shape[-1])
    col = lambda p: jnp.pad(p, ((0, 0), (0, H))).reshape(depth, 2 * H, 1)
    w_in_t = jnp.swapaxes(w_in, 1, 2)
    params = (
        row(norm_mix),
        w_in_t,
        w_in_t[:, MAIN_WIDTH:, :].astype(BF16),
        conv_w,
        w_pool.astype(BF16),
        row(pool_scale),
        col(a_log),
        col(dt_bias),
        row(gdn_norm),
        w_out,
        row(norm_mlp),
        w_up,
        w_down,
    )
    nf = norm_final.reshape(1, D_MODEL)
    h = x
    for l in range(depth):
        h = _layer(l, h, params, nf, final_norm=(l == depth - 1))
    return h
```

```python
import functools

import jax
import jax.numpy as jnp
from jax import lax
from jax.experimental import pallas as pl
from jax.experimental.pallas import tpu as pltpu

D_MODEL = 1024
CHUNK = 64
POOL_WIDTH = D_MODEL // 2
POOL_GROUPS = 4
POOL_GROUP_DIM = POOL_WIDTH // POOL_GROUPS
POOL_WINDOWS = (2, 4, 8, 16)
GDN_WIDTH = D_MODEL - POOL_WIDTH
GDN_HEADS = 4
GDN_HEAD_DIM = GDN_WIDTH // GDN_HEADS
CONV_WIDTH = 4
D_FF = 4 * D_MODEL
EPS = 1e-6

SUBLANES = 8
MAIN_WIDTH = POOL_WIDTH + 4 * GDN_WIDTH
PROJ_WIDTH = MAIN_WIDTH
QKV_OFF = POOL_WIDTH
Z_OFF = POOL_WIDTH + 3 * GDN_WIDTH
CONV_HALO = SUBLANES
POOL_HALO = 16
INV_BASE = 16

TS = 256
NC = TS // CHUNK
FF_CHUNK = 512
PROJ_CHUNK = 256
WEIGHT_CHUNK_BYTES = 2 * 1024 * 1024
WEIGHT_SLOTS = 4
VMEM_LIMIT = 58 * 1024 * 1024

F32 = jnp.float32
BF16 = jnp.bfloat16


def _rms_norm(x, w):
    ms = jnp.mean(x * x, axis=-1, keepdims=True)
    return x * lax.rsqrt(ms + EPS) * w


def _softplus(x):
    return jnp.maximum(x, 0.0) + jnp.log1p(jnp.exp(-jnp.abs(x)))


def _sigmoid(x):
    return 1.0 / (1.0 + jnp.exp(-x))


def _bdot(a, b):
    return jnp.einsum('cik,ckj->cij', a.astype(BF16), b.astype(BF16),
                      preferred_element_type=F32)


def _bdot_nt(a, b):
    return jnp.einsum('cid,cjd->cij', a.astype(BF16), b.astype(BF16),
                      preferred_element_type=F32)


def _unit_lower_inverse(m, fill):
    half = m.shape[0] // 2
    m = jnp.concatenate([m[:half], m[half:]], axis=2)
    ri = lax.broadcasted_iota(jnp.int32, m.shape, 1)
    lane = lax.broadcasted_iota(jnp.int32, m.shape, 2)
    ci = lane % CHUNK
    first = lane < CHUNK

    def pdot(x, y):
        y_diag = jnp.concatenate([jnp.where(first, y, 0.0), jnp.where(first, 0.0, y)], axis=1)
        return _bdot(x, y_diag)

    eye = (ri == ci).astype(F32)
    d = jnp.where(ri // INV_BASE == ci // INV_BASE, m, 0.0)
    t = eye - d
    p = pdot(d, d)
    fill()
    power = 2
    while power < INV_BASE:
        t = t + pdot(t, p)
        power *= 2
        if power < INV_BASE:
            p = pdot(p, p)
        fill()
    b = INV_BASE
    while b < CHUNK:
        lower_left = ((ri // (2 * b) == ci // (2 * b))
                      & ((ri // b) % 2 == 1) & ((ci // b) % 2 == 0))
        c = jnp.where(lower_left, m, 0.0)
        ct = pdot(c, t)
        fill()
        t = t - pdot(t, ct)
        b *= 2
        fill()
    return jnp.concatenate([t[:, :, :CHUNK], t[:, :, CHUNK:]], axis=0)


def _inproj_pieces(x_ref, nw_ref, w_ref, wgt_ref, proj_dst, gt_dst, h_dst, u_ref):
    nt = (((1,), (1,)), ((), ()))

    def prepare():
        x = x_ref[...]
        h_dst[...] = x
        u_ref[...] = _rms_norm(x, nw_ref[...]).astype(BF16)

    def project(cols):
        def piece():
            proj_dst[:, cols] = jnp.dot(u_ref[...], w_ref[:, cols], preferred_element_type=F32)
        return piece

    def gates():
        gt = lax.dot_general(wgt_ref[...], u_ref[...], nt,
                             preferred_element_type=F32)
        for c in range(NC):
            gt_dst[c] = gt[:, c * CHUNK:(c + 1) * CHUNK]

    starts = range(0, PROJ_WIDTH, PROJ_CHUNK)
    return ([prepare] + [project(slice(c0, min(c0 + PROJ_CHUNK, PROJ_WIDTH))) for c0 in starts]
            + [gates])


def _mixer_stage(proj_ref, gt_ref, h_ref, tile_in_row, convw_ref, wpool_ref, pscale_ref,
                 alog_col_ref, dtb_col_ref, gnorm_ref, wout_ref,
                 hmid_dst, state_ref, conv_ref, pool_ref, mix_ref, fill):
    gate_r = gt_ref[...].reshape(NC * 2 * GDN_HEADS, CHUNK)
    alog_col = jnp.concatenate([alog_col_ref[...]] * NC, axis=0)
    dtb_col = jnp.concatenate([dtb_col_ref[...]] * NC, axis=0)
    g_r = -jnp.exp(alog_col) * _softplus(gate_r + dtb_col)
    beta_r = _sigmoid(gate_r)

    conv_ref[CONV_HALO:CONV_HALO + TS, :] = proj_ref[:, QKV_OFF:QKV_OFF + 3 * GDN_WIDTH]
    qkv_blocks = []
    half = TS // 2
    for blk in range(3):
        cs = slice(blk * GDN_WIDTH, (blk + 1) * GDN_WIDTH)
        parts = []
        for r0 in (CONV_HALO, CONV_HALO + half):
            y = conv_ref[r0:r0 + half, cs] * convw_ref[CONV_WIDTH - 1:CONV_WIDTH, cs]
            for j in range(1, CONV_WIDTH):
                y = y + (conv_ref[r0 - j:r0 - j + half, cs]
                         * convw_ref[CONV_WIDTH - 1 - j:CONV_WIDTH - j, cs])
            parts.append(y * _sigmoid(y))
            fill()
        qkv_blocks.append(jnp.concatenate(parts, axis=0))
    conv_ref[0:CONV_HALO, :] = conv_ref[TS:TS + CONV_HALO, :]

    pool_ref[POOL_HALO:POOL_HALO + TS, :] = proj_ref[:, 0:POOL_WIDTH]
    count = (lax.broadcasted_iota(jnp.int32, (TS, POOL_GROUP_DIM), 0)
             + (tile_in_row * TS + 1)).astype(F32)
    for g, w in enumerate(POOL_WINDOWS):
        sl = slice(g * POOL_GROUP_DIM, (g + 1) * POOL_GROUP_DIM)
        x0 = pool_ref[POOL_HALO:POOL_HALO + TS, sl]
        acc = x0
        for j in range(1, w):
            acc = acc + pool_ref[POOL_HALO - j:POOL_HALO - j + TS, sl]
        pooled = acc / jnp.minimum(count, float(w)) - x0
        mix_ref[:, sl] = pooled.astype(BF16)
    pool_ref[0:POOL_HALO, :] = pool_ref[TS:TS + POOL_HALO, :]
    fill()

    def pool_map(g):
        sl = slice(g * POOL_GROUP_DIM, (g + 1) * POOL_GROUP_DIM)
        mixed = jnp.dot(mix_ref[:, sl], wpool_ref[g], preferred_element_type=F32)
        mix_ref[:, sl] = (mixed * pscale_ref[:, sl]).astype(BF16)

    u_i = lax.broadcasted_iota(jnp.int32, (CHUNK, CHUNK), 0)
    v_i = lax.broadcasted_iota(jnp.int32, (CHUNK, CHUNK), 1)
    upper = (u_i <= v_i).astype(F32)
    gcum_r = jnp.dot(g_r, upper, preferred_element_type=F32,
                     precision=lax.Precision.HIGHEST)
    gates_t = jnp.concatenate([gcum_r, beta_r], axis=1).T
    gcum_r = gcum_r.reshape(NC, 2 * GDN_HEADS, CHUNK)

    H = GDN_HEADS
    dh = GDN_HEAD_DIM

    def heads(x2d):
        return jnp.stack([x2d[:, hd * dh:(hd + 1) * dh] for hd in range(H)],
                         axis=0).reshape(H * NC, CHUNK, dh)

    def head_cols(row0, off):
        return jnp.stack([gates_t[row0:row0 + CHUNK, c * 2 * H + off + hd:c * 2 * H + off + hd + 1]
                          for hd in range(H) for c in range(NC)], axis=0)

    def head_rows(x3d, off):
        return jnp.stack([x3d[:, off + hd:off + hd + 1, :] for hd in range(H)],
                         axis=0).reshape(H * NC, 1, CHUNK)

    fill()
    q = heads(qkv_blocks[0])
    k = heads(qkv_blocks[1])
    v = heads(qkv_blocks[2])
    qh = q * (lax.rsqrt(jnp.sum(q * q, axis=-1, keepdims=True) + EPS) * (dh ** -0.5))
    kh = k * lax.rsqrt(jnp.sum(k * k, axis=-1, keepdims=True) + EPS)

    gc = head_cols(0, 0)
    bc = head_cols(CHUNK, H)
    gr = head_rows(gcum_r, 0)
    g_last = gc[:, CHUNK - 1:CHUNK, :]
    eg = jnp.exp(gc)

    ri = lax.broadcasted_iota(jnp.int32, (H * NC, CHUNK, CHUNK), 1)
    ci = lax.broadcasted_iota(jnp.int32, (H * NC, CHUNK, CHUNK), 2)
    decay_incl = jnp.exp(jnp.where(ri >= ci, gc - gr, -jnp.inf))
    decay_strict = jnp.where(ri > ci, decay_incl, 0.0)

    kh_b = kh.astype(BF16)
    m = _bdot_nt(kh_b, kh_b) * bc * decay_strict
    fill()
    t_inv = _unit_lower_inverse(m, fill)
    rhs = jnp.concatenate([v * bc, kh * (bc * eg)], axis=-1)
    sol = _bdot(t_inv, rhs)
    fill()
    per_chunk = lambda x: x.reshape((H, NC) + x.shape[1:])
    value = per_chunk(sol[:, :, :dh])
    k_cumdecay = per_chunk(sol[:, :, dh:])
    attn = per_chunk(_bdot_nt(qh, kh_b) * decay_incl)
    q_dec = per_chunk(qh * eg)
    k_dec = per_chunk(kh * jnp.exp(g_last - gc))
    state_decay = per_chunk(jnp.exp(g_last))
    fill()

    state = state_ref[...]
    outs = []
    for c in range(NC):
        lhs = jnp.concatenate([k_cumdecay[:, c], q_dec[:, c]], axis=1)
        r = _bdot(lhs, state)
        fill()
        v_new = (value[:, c] - r[:, :CHUNK]).astype(BF16)
        outs.append(r[:, CHUNK:] + _bdot(attn[:, c], v_new))
        state = state * state_decay[:, c] + jnp.einsum(
            'hcd,hce->hde', k_dec[:, c].astype(BF16), v_new, preferred_element_type=F32)
        if c < POOL_GROUPS:
            pool_map(c)
        fill()
    for g in range(NC, POOL_GROUPS):
        pool_map(g)
    state_ref[...] = state

    o = jnp.concatenate(outs, axis=1)
    o = o * lax.rsqrt(jnp.mean(o * o, axis=-1, keepdims=True) + EPS) * gnorm_ref[...]
    for hd in range(H):
        z = proj_ref[:, Z_OFF + hd * dh:Z_OFF + (hd + 1) * dh]
        mix_ref[:, POOL_WIDTH + hd * dh:POOL_WIDTH + (hd + 1) * dh] = (
            o[hd] * (z * _sigmoid(z))).astype(BF16)

    hmid_dst[...] = h_ref[...] + jnp.dot(mix_ref[...], wout_ref[...],
                                         preferred_element_type=F32)


def _mlp_pieces(x_ref, nw_ref, wup_ref, wdown_ref, nf_ref, out_ref, u_ref, hid_ref, acc_ref,
                final_norm):
    n = D_FF // FF_CHUNK

    def prepare():
        x = x_ref[...]
        acc_ref[...] = x
        u_ref[...] = _rms_norm(x, nw_ref[...]).astype(BF16)

    def up(j):
        def piece():
            cols = slice(j * FF_CHUNK, (j + 1) * FF_CHUNK)
            hid = jnp.dot(u_ref[...], wup_ref[:, cols], preferred_element_type=F32)
            hid_ref[j % 2] = jnp.square(jnp.maximum(hid, 0.0)).astype(BF16)
        return piece

    def down(j):
        def piece():
            rows = slice(j * FF_CHUNK, (j + 1) * FF_CHUNK)
            acc_ref[...] += jnp.dot(hid_ref[j % 2], wdown_ref[rows, :],
                                    preferred_element_type=F32)
        return piece

    def finish():
        acc = acc_ref[...]
        out_ref[0] = _rms_norm(acc, nf_ref[...]) if final_norm else acc

    order = [prepare, up(0)]
    for j in range(1, n):
        order += [up(j), down(j - 1)]
    return order + [down(n - 1), finish]


def _stream_weight(w_hbm, layer, rows_per_chunk, n_chunks, n_cols, consume):
    def scoped(stage, sem):
        def copy(i):
            return pltpu.make_async_copy(
                w_hbm.at[layer, pl.ds(i * rows_per_chunk, rows_per_chunk), pl.ds(0, n_cols)],
                stage.at[i % WEIGHT_SLOTS], sem.at[i % WEIGHT_SLOTS])

        for i in range(min(WEIGHT_SLOTS - 1, n_chunks)):
            copy(i).start()
        for i in range(n_chunks):
            if i + WEIGHT_SLOTS - 1 < n_chunks:
                copy(i + WEIGHT_SLOTS - 1).start()
            copy(i).wait()
            consume(i, stage.at[i % WEIGHT_SLOTS])

    pl.run_scoped(scoped, pltpu.VMEM((WEIGHT_SLOTS, rows_per_chunk, n_cols), F32),
                  pltpu.SemaphoreType.DMA((WEIGHT_SLOTS,)))


def _load_weight(w_hbm, layer, dst):
    n_rows, n_cols = dst.shape
    rows = 1 << ((WEIGHT_CHUNK_BYTES // (4 * n_cols)).bit_length() - 1)
    assert n_rows % rows == 0 and rows % SUBLANES == 0

    def consume(i, chunk):
        dst[i * rows:(i + 1) * rows, :] = chunk[...].astype(BF16)

    _stream_weight(w_hbm, layer, rows, n_rows // rows, n_cols, consume)


def _load_weight_transposed(wt_hbm, layer, dst):
    n_in, n_out = dst.shape
    rows = PROJ_CHUNK
    assert n_out % rows == 0

    def consume(i, chunk):
        dst[:, i * rows:(i + 1) * rows] = chunk[...].T.astype(BF16)

    _stream_weight(wt_hbm, layer, rows, n_out // rows, n_in, consume)


def _layer_kernel(h_next_ref, h_first_ref, nwmix_ref, wint_hbm, wgt_ref, convw_ref, wpool_ref,
                  pscale_ref, alog_col_ref, dtb_col_ref, gnorm_ref,
                  wout_hbm, nwmlp_ref, wup_hbm, wdown_hbm, nf_ref, out_ref,
                  proj_scr, gt_scr, h_scr, hmid_scr, state_ref, conv_ref, pool_ref, mix_ref,
                  uin_scr, umlp_scr, hid_scr, acc_scr, win_ref, wout_ref, wup_ref, wdown_ref,
                  *, layer, tiles_per_row, final_norm):
    s = pl.program_id(0)
    cur = s % 2
    nxt = 1 - cur

    @pl.when(s == 0)
    def _():
        _load_weight_transposed(wint_hbm, layer, win_ref)
        for w_hbm, dst in ((wout_hbm, wout_ref), (wup_hbm, wup_ref), (wdown_hbm, wdown_ref)):
            _load_weight(w_hbm, layer, dst)
        for piece in _inproj_pieces(h_first_ref.at[0], nwmix_ref, win_ref, wgt_ref,
                                    proj_scr.at[0], gt_scr.at[0], h_scr.at[0], uin_scr):
            piece()
        hmid_scr[1] = jnp.zeros((TS, D_MODEL), F32)

    @pl.when(s % tiles_per_row == 0)
    def _():
        state_ref[...] = jnp.zeros_like(state_ref)
        conv_ref[0:CONV_HALO, :] = jnp.zeros((CONV_HALO, 3 * GDN_WIDTH), F32)
        pool_ref[0:POOL_HALO, :] = jnp.zeros((POOL_HALO, POOL_WIDTH), F32)

    mlp = _mlp_pieces(hmid_scr.at[nxt], nwmlp_ref, wup_ref, wdown_ref, nf_ref, out_ref,
                      umlp_scr, hid_scr, acc_scr, final_norm)
    inp = _inproj_pieces(h_next_ref.at[0], nwmix_ref, win_ref, wgt_ref,
                         proj_scr.at[nxt], gt_scr.at[nxt], h_scr.at[nxt], uin_scr)
    dense = iter(mlp[:1] + inp[:1] + mlp[1:-1] + inp[1:-1] + mlp[-1:] + inp[-1:])

    def fill():
        piece = next(dense, None)
        if piece is not None:
            piece()

    fill()
    fill()
    _mixer_stage(proj_scr.at[cur], gt_scr.at[cur], h_scr.at[cur], s % tiles_per_row,
                 convw_ref, wpool_ref, pscale_ref, alog_col_ref,
                 dtb_col_ref, gnorm_ref, wout_ref, hmid_scr.at[cur],
                 state_ref, conv_ref, pool_ref, mix_ref, fill)
    for piece in dense:
        piece()


def _layer(layer, h, params, nf, final_norm):
    B, S, D = h.shape
    tiles_per_row = S // TS
    n_tiles = B * tiles_per_row

    def tile_map(offset):
        def index_map(s):
            t = jnp.clip(s + offset, 0, n_tiles - 1)
            return (t // tiles_per_row, t % tiles_per_row, 0)
        return index_map

    def layer_spec(p):
        block = tuple(p.shape[1:])
        zeros = (0,) * len(block)
        return pl.BlockSpec((None,) + block, lambda s: (layer,) + zeros,
                            pipeline_mode=pl.Buffered(1))

    in_hbm = (1, 9, 11, 12)
    param_specs = [pl.BlockSpec(memory_space=pl.ANY) if i in in_hbm else layer_spec(p)
                   for i, p in enumerate(params)]

    return pl.pallas_call(
        functools.partial(_layer_kernel, layer=layer, tiles_per_row=tiles_per_row,
                          final_norm=final_norm),
        grid=(n_tiles + 1,),
        in_specs=[
            pl.BlockSpec((1, TS, D), tile_map(1)),
            pl.BlockSpec((1, TS, D), lambda s: (0, 0, 0), pipeline_mode=pl.Buffered(1)),
            *param_specs,
            pl.BlockSpec((1, D), lambda s: (0, 0), pipeline_mode=pl.Buffered(1)),
        ],
        out_specs=pl.BlockSpec((1, TS, D), tile_map(-1)),
        out_shape=jax.ShapeDtypeStruct((B, S, D), F32),
        scratch_shapes=[
            pltpu.VMEM((2, TS, PROJ_WIDTH), F32),
            pltpu.VMEM((2, NC, 2 * GDN_HEADS, CHUNK), F32),
            pltpu.VMEM((2, TS, D), F32),
            pltpu.VMEM((2, TS, D), F32),
            pltpu.VMEM((GDN_HEADS, GDN_HEAD_DIM, GDN_HEAD_DIM), F32),
            pltpu.VMEM((TS + CONV_HALO, 3 * GDN_WIDTH), F32),
            pltpu.VMEM((TS + POOL_HALO, POOL_WIDTH), F32),
            pltpu.VMEM((TS, D), BF16),
            pltpu.VMEM((TS, D), BF16),
            pltpu.VMEM((TS, D), BF16),
            pltpu.VMEM((2, TS, FF_CHUNK), BF16),
            pltpu.VMEM((TS, D), F32),
            pltpu.VMEM((D, PROJ_WIDTH), BF16),
            pltpu.VMEM((D, D), BF16),
            pltpu.VMEM((D, D_FF), BF16),
            pltpu.VMEM((D_FF, D), BF16),
        ],
        compiler_params=pltpu.CompilerParams(
            dimension_semantics=("arbitrary",),
            vmem_limit_bytes=VMEM_LIMIT),
        name="layer_final" if final_norm else "layer",
    )(h, h, *params, nf)


def kernel(x, norm_mix, w_in, conv_w, w_pool, pool_scale, a_log, dt_bias, gdn_norm, w_out,
           norm_mlp, w_up, w_down, norm_final):
    depth = w_in.shape[0]
    H = GDN_HEADS
    row = lambda p: p.reshape(depth, 1, p.shape[-1])
    col = lambda p: jnp.pad(p, ((0, 0), (0, H))).reshape(depth, 2 * H, 1)
    w_in_t = jnp.swapaxes(w_in, 1, 2)
    params = (
        row(norm_mix),
        w_in_t,
        w_in_t[:, MAIN_WIDTH:, :].astype(BF16),
        conv_w,
        w_pool.astype(BF16),
        row(pool_scale),
        col(a_log),
        col(dt_bias),
        row(gdn_norm),
        w_out,
        row(norm_mlp),
        w_up,
        w_down,
    )
    nf = norm_final.reshape(1, D_MODEL)
    h = x
    for l in range(depth):
        h = _layer(l, h, params, nf, final_norm=(l == depth - 1))
    return h
```
